```python
import jax, jax.numpy as jnp
from jax import lax
import numpy as np

D_MODEL = 2048
BATCH = 8
SEQ = 4096
DEPTH = 2
DEC_BATCH = 1
DEC_SEQ = 8192
PAST_LEN = 128

HEAD_DIM = 128
FOURIER_WIDTH = D_MODEL // 4
N_FOURIER_GROUPS = FOURIER_WIDTH // HEAD_DIM
ATTN_WIDTH = D_MODEL // 2
N_ATTN_HEADS = ATTN_WIDTH // HEAD_DIM
MEM_WIDTH = D_MODEL // 4
N_MEM_HEADS = 4
MEM_HEAD_DIM = MEM_WIDTH // N_MEM_HEADS
MIX_WIDTH = FOURIER_WIDTH + ATTN_WIDTH + MEM_WIDTH
IN_WIDTH = FOURIER_WIDTH + 3 * ATTN_WIDTH + MEM_WIDTH
N_MEM = 256
D_FF = 5632
DILATION_PAIRS = ((128, 1), (512, 4), (2048, 16))
BAND_BLOCK = 64
ROPE_THETA = 10000.0
EPS = 1e-6

kernel_name = "hybrid_fnet_dilated_memory_macaron_encoder"


def rmsnorm(x, g):
    xf = x.astype(jnp.float32)
    y = xf * lax.rsqrt(jnp.mean(xf * xf, axis=-1, keepdims=True) + EPS)
    return (y * g.astype(jnp.float32)).astype(x.dtype)


def swiglu(h, w_gate, w_up, w_down):
    return (jax.nn.silu(h @ w_gate) * (h @ w_up)) @ w_down


def rope(x):
    S, hd = x.shape[1], x.shape[-1]
    inv = ROPE_THETA ** (-jnp.arange(0, hd, 2, dtype=jnp.float32) / hd)
    ang = jnp.arange(S, dtype=jnp.float32)[:, None] * inv[None, :]
    cos = jnp.cos(ang)[None, :, None, :]
    sin = jnp.sin(ang)[None, :, None, :]
    xf = x.astype(jnp.float32)
    x1, x2 = xf[..., : hd // 2], xf[..., hd // 2:]
    return jnp.concatenate([x1 * cos - x2 * sin, x2 * cos + x1 * sin], axis=-1).astype(x.dtype)


def band_attention(q, k, v, half_window):
    G, L, H, hd = q.shape
    nb = -(-L // BAND_BLOCK)
    pad = nb * BAND_BLOCK - L
    qb = jnp.pad(q, ((0, 0), (0, pad), (0, 0), (0, 0))).reshape(G, nb, BAND_BLOCK, H, hd)

    def windows(t):
        tp = jnp.pad(t, ((0, 0), (BAND_BLOCK, pad + BAND_BLOCK), (0, 0), (0, 0)))
        tp = tp.reshape(G, nb + 2, BAND_BLOCK, H, hd)
        return jnp.concatenate([tp[:, :-2], tp[:, 1:-1], tp[:, 2:]], axis=2)

    kw, vw = windows(k), windows(v)
    s = jnp.einsum('gnqhd,gnkhd->gnhqk', qb, kw, preferred_element_type=jnp.float32) * (hd ** -0.5)
    q_pos = jnp.arange(nb)[:, None] * BAND_BLOCK + jnp.arange(BAND_BLOCK)[None, :]
    k_pos = (jnp.arange(nb)[:, None] - 1) * BAND_BLOCK + jnp.arange(3 * BAND_BLOCK)[None, :]
    dist = jnp.abs(q_pos[:, :, None] - k_pos[:, None, :])
    valid = (dist <= half_window) & (k_pos[:, None, :] >= 0) & (k_pos[:, None, :] < L)
    s = jnp.where(valid[None, :, None], s, -jnp.inf)
    lse = jax.nn.logsumexp(s, axis=-1)
    p = jnp.exp(s - lse[..., None])
    o = jnp.einsum('gnhqk,gnkhd->gnqhd', p.astype(v.dtype), vw)
    o = o.reshape(G, nb * BAND_BLOCK, H, hd)[:, :L]
    lse = lse.transpose(0, 1, 3, 2).reshape(G, nb * BAND_BLOCK, H)[:, :L]
    return o, lse


def dilated_attention(q, k, v):
    B, S, H, hd = q.shape
    outs, lses = [], []
    for window, dil in DILATION_PAIRS:
        L = S // dil
        half = window // (2 * dil)

        def to_classes(t):
            return t.reshape(B, L, dil, H, hd).transpose(0, 2, 1, 3, 4).reshape(B * dil, L, H, hd)

        o, lse = band_attention(to_classes(q), to_classes(k), to_classes(v), half)
        outs.append(o.reshape(B, dil, L, H, hd).transpose(0, 2, 1, 3, 4).reshape(B, S, H, hd))
        lses.append(lse.reshape(B, dil, L, H).transpose(0, 2, 1, 3).reshape(B, S, H))
    wts = jax.nn.softmax(jnp.stack(lses, axis=0), axis=0)
    out = jnp.einsum('pbsh,pbshd->bshd', wts, jnp.stack(outs, axis=0).astype(jnp.float32))
    return out.astype(q.dtype)


def memory_attention(mq, mem_h, w_mem_kv, g_mq, g_mk):
    B, M, _ = mem_h.shape
    kv = mem_h @ w_mem_kv
    mk = rmsnorm(kv[..., :MEM_WIDTH].reshape(B, M, N_MEM_HEADS, MEM_HEAD_DIM), g_mk)
    mv = kv[..., MEM_WIDTH:].reshape(B, M, N_MEM_HEADS, MEM_HEAD_DIM)
    mq = rmsnorm(mq, g_mq)
    s = jnp.einsum('bshd,bmhd->bhsm', mq, mk, preferred_element_type=jnp.float32) * (MEM_HEAD_DIM ** -0.5)
    p = jax.nn.softmax(s, axis=-1)
    return jnp.einsum('bhsm,bmhd->bshd', p.astype(mv.dtype), mv)


def fourier_mix(f_in, w_fourier):
    B, S, _ = f_in.shape
    f = f_in.reshape(B, S, N_FOURIER_GROUPS, HEAD_DIM).astype(jnp.float32)
    f = jnp.fft.fft2(f, axes=(1, 3), norm='ortho').real
    return f.reshape(B, S, FOURIER_WIDTH).astype(f_in.dtype) @ w_fourier


def trunk(x, mem, g_ffn1, w_ffn1_gate, w_ffn1_up, w_ffn1_down, g_mix, w_in, g_q, g_k,
          g_mem, w_mem_kv, g_mq, g_mk, w_fourier, g_out, w_out,
          g_ffn2, w_ffn2_gate, w_ffn2_up, w_ffn2_down):
    B, S, _ = x.shape
    a0, a1 = FOURIER_WIDTH, FOURIER_WIDTH + ATTN_WIDTH
    o0, o1 = FOURIER_WIDTH, FOURIER_WIDTH + ATTN_WIDTH
    for l in range(DEPTH):
        x = x + 0.5 * swiglu(rmsnorm(x, g_ffn1[l]), w_ffn1_gate[l], w_ffn1_up[l], w_ffn1_down[l])
        h = rmsnorm(x, g_mix[l])
        proj = h @ w_in[l]
        f_in = proj[..., :a0]
        q = proj[..., a0:a0 + ATTN_WIDTH].reshape(B, S, N_ATTN_HEADS, HEAD_DIM)
        k = proj[..., a0 + ATTN_WIDTH:a0 + 2 * ATTN_WIDTH].reshape(B, S, N_ATTN_HEADS, HEAD_DIM)
        v = proj[..., a0 + 2 * ATTN_WIDTH:a0 + 3 * ATTN_WIDTH].reshape(B, S, N_ATTN_HEADS, HEAD_DIM)
        mq = proj[..., a0 + 3 * ATTN_WIDTH:].reshape(B, S, N_MEM_HEADS, MEM_HEAD_DIM)

        y_f = fourier_mix(f_in, w_fourier[l])
        q = rope(rmsnorm(q, g_q[l]))
        k = rope(rmsnorm(k, g_k[l]))
        y_a = dilated_attention(q, k, v).reshape(B, S, ATTN_WIDTH)
        y_m = memory_attention(mq, rmsnorm(mem, g_mem[l]), w_mem_kv[l], g_mq[l], g_mk[l])
        y_m = y_m.reshape(B, S, MEM_WIDTH)

        go = g_out[l]
        y = jnp.concatenate([rmsnorm(y_f, go[:o0]), rmsnorm(y_a, go[o0:o1]), rmsnorm(y_m, go[o1:])], axis=-1)
        x = x + y @ w_out[l]
        x = x + 0.5 * swiglu(rmsnorm(x, g_ffn2[l]), w_ffn2_gate[l], w_ffn2_up[l], w_ffn2_down[l])
    return x


def setup_inputs(seed: int = 0) -> dict:
    key = jax.random.key(seed)
    ks = jax.random.split(key, 24)

    def w(k, shape, fan_in):
        return jax.random.normal(k, shape, jnp.float32) * (fan_in ** -0.5)

    def gain(k, shape):
        return 1.0 + 0.02 * jax.random.normal(k, shape, jnp.float32)

    return {
        'x_prompt': jax.random.normal(ks[0], (BATCH, SEQ, D_MODEL), jnp.float32),
        'x_sample': jax.random.normal(ks[1], (DEC_BATCH, DEC_SEQ, D_MODEL), jnp.float32),
        'mem_prompt': jax.random.normal(ks[2], (BATCH, N_MEM, D_MODEL), jnp.float32),
        'mem_sample': jax.random.normal(ks[3], (DEC_BATCH, N_MEM, D_MODEL), jnp.float32),
        'g_ffn1': gain(ks[4], (DEPTH, D_MODEL)),
        'w_ffn1_gate': w(ks[5], (DEPTH, D_MODEL, D_FF), D_MODEL),
        'w_ffn1_up': w(ks[6], (DEPTH, D_MODEL, D_FF), D_MODEL),
        'w_ffn1_down': w(ks[7], (DEPTH, D_FF, D_MODEL), D_FF),
        'g_mix': gain(ks[8], (DEPTH, D_MODEL)),
        'w_in': w(ks[9], (DEPTH, D_MODEL, IN_WIDTH), D_MODEL),
        'g_q': gain(ks[10], (DEPTH, HEAD_DIM)),
        'g_k': gain(ks[11], (DEPTH, HEAD_DIM)),
        'g_mem': gain(ks[12], (DEPTH, D_MODEL)),
        'w_mem_kv': w(ks[13], (DEPTH, D_MODEL, 2 * MEM_WIDTH), D_MODEL),
        'g_mq': gain(ks[14], (DEPTH, MEM_HEAD_DIM)),
        'g_mk': gain(ks[15], (DEPTH, MEM_HEAD_DIM)),
        'w_fourier': w(ks[16], (DEPTH, FOURIER_WIDTH, FOURIER_WIDTH), FOURIER_WIDTH),
        'g_out': gain(ks[17], (DEPTH, MIX_WIDTH)),
        'w_out': w(ks[18], (DEPTH, MIX_WIDTH, D_MODEL), MIX_WIDTH),
        'g_ffn2': gain(ks[19], (DEPTH, D_MODEL)),
        'w_ffn2_gate': w(ks[20], (DEPTH, D_MODEL, D_FF), D_MODEL),
        'w_ffn2_up': w(ks[21], (DEPTH, D_MODEL, D_FF), D_MODEL),
        'w_ffn2_down': w(ks[22], (DEPTH, D_FF, D_MODEL), D_FF),
    }


def reference(x_prompt, x_sample, mem_prompt, mem_sample, g_ffn1, w_ffn1_gate, w_ffn1_up,
              w_ffn1_down, g_mix, w_in, g_q, g_k, g_mem, w_mem_kv, g_mq, g_mk, w_fourier,
              g_out, w_out, g_ffn2, w_ffn2_gate, w_ffn2_up, w_ffn2_down):
    y_prompt = trunk(x_prompt, mem_prompt, g_ffn1, w_ffn1_gate, w_ffn1_up, w_ffn1_down, g_mix,
                     w_in, g_q, g_k, g_mem, w_mem_kv, g_mq, g_mk, w_fourier, g_out, w_out,
                     g_ffn2, w_ffn2_gate, w_ffn2_up, w_ffn2_down)
    y_sample = trunk(x_sample, mem_sample, g_ffn1, w_ffn1_gate, w_ffn1_up, w_ffn1_down, g_mix,
                     w_in, g_q, g_k, g_mem, w_mem_kv, g_mq, g_mk, w_fourier, g_out, w_out,
                     g_ffn2, w_ffn2_gate, w_ffn2_up, w_ffn2_down)
    return (y_prompt, y_sample)
```

```python
import functools
import math

import jax
import jax.numpy as jnp
from jax import lax
from jax.experimental import pallas as pl
from jax.experimental.pallas import tpu as pltpu

EPS = 1e-6
HEAD_DIM = 128
ROPE_THETA = 10000.0
DILATIONS = (1, 4, 16)
HALF_WINDOW = 64
Q_BLOCK = 2 * HALF_WINDOW
K_WINDOW = 2 * Q_BLOCK
FFT_N1 = 64
NEG_BIG = -1e30

BF16 = jnp.bfloat16
F32 = jnp.float32

VMEM_LIMIT_BYTES = 56 * 1024 * 1024


def _params(*semantics):
    return pltpu.CompilerParams(dimension_semantics=semantics, vmem_limit_bytes=VMEM_LIMIT_BYTES)


def _rms(x, g):
    return x * lax.rsqrt(jnp.mean(x * x, axis=-1, keepdims=True) + EPS) * g


def _dot(a, b):
    return jnp.dot(a, b, preferred_element_type=F32)


def _ffn_kernel(x_ref, g_ref, wg_ref, wu_ref, wd_ref, o_ref, h_ref):
    f = pl.program_id(1)

    @pl.when(f == 0)
    def _():
        x = x_ref[...]
        h_ref[...] = _rms(x, g_ref[...]).astype(BF16)
        o_ref[...] = x

    h = h_ref[...]
    gate = _dot(h, wg_ref[...])
    up = _dot(h, wu_ref[...])
    a = (0.5 * gate * (1.0 / (1.0 + jnp.exp(-gate))) * up).astype(BF16)
    o_ref[...] += _dot(a, wd_ref[...])


def _ffn(x, g, wg, wu, wd, *, tm=512, tf=512):
    T, D = x.shape
    F = wg.shape[1]
    tf = min(tf, F)
    assert T % tm == 0 and F % tf == 0
    return pl.pallas_call(
        _ffn_kernel,
        grid=(T // tm, F // tf),
        in_specs=[
            pl.BlockSpec((tm, D), lambda i, f: (i, 0)),
            pl.BlockSpec((1, D), lambda i, f: (0, 0)),
            pl.BlockSpec((D, tf), lambda i, f: (0, f)),
            pl.BlockSpec((D, tf), lambda i, f: (0, f)),
            pl.BlockSpec((tf, D), lambda i, f: (f, 0)),
        ],
        out_specs=pl.BlockSpec((tm, D), lambda i, f: (i, 0)),
        out_shape=jax.ShapeDtypeStruct((T, D), F32),
        scratch_shapes=[pltpu.VMEM((tm, D), BF16)],
        compiler_params=_params("parallel", "arbitrary"),
        name="ffn",
    )(x, g.reshape(1, D), wg, wu, wd)


def _inproj_kernel(x_ref, g_ref, wf_ref, wq_ref, wk_ref, wv_ref, wm_ref, gq_ref, gk_ref, gmq_ref,
                   cos_ref, sin_ref, f_ref, q_ref, k_ref, v_ref, mq_ref, *, col_chunk):
    h = _rms(x_ref[...], g_ref[...]).astype(BF16)
    cos = cos_ref[...]
    sin = sin_ref[...]

    f_ref[...] = _dot(h, wf_ref[...])

    def heads(w_ref, out_ref, fn):
        width = w_ref.shape[1]
        cc = min(col_chunk, width)
        for c0 in range(0, width, cc):
            p = _dot(h, w_ref[:, c0:c0 + cc])
            for j in range(cc // HEAD_DIM):
                sl = slice(c0 + j * HEAD_DIM, c0 + (j + 1) * HEAD_DIM)
                out_ref[:, sl] = fn(p[:, j * HEAD_DIM:(j + 1) * HEAD_DIM]).astype(BF16)

    def norm_rope(g_ref_):
        def fn(t):
            t = _rms(t, g_ref_[...])
            return t * cos + pltpu.roll(t, HEAD_DIM // 2, 1) * sin
        return fn

    heads(wq_ref, q_ref, norm_rope(gq_ref))
    heads(wk_ref, k_ref, norm_rope(gk_ref))
    heads(wv_ref, v_ref, lambda t: t)
    heads(wm_ref, mq_ref, lambda t: _rms(t, gmq_ref[...]))


def _inproj(x, g, wf, wq, wk, wv, wm, gq, gk, gmq, cos2, sin2, *, tm=256):
    B, S, D = x.shape
    FW, AW, MW = wf.shape[1], wq.shape[1], wm.shape[1]
    assert S % tm == 0
    row = lambda w: pl.BlockSpec((None, tm, w), lambda b, i: (b, i, 0))
    full = lambda a: pl.BlockSpec(a.shape, lambda b, i: (0,) * a.ndim)
    vec = lambda n: pl.BlockSpec((1, n), lambda b, i: (0, 0))
    tab = pl.BlockSpec((tm, HEAD_DIM), lambda b, i: (i, 0))
    return pl.pallas_call(
        functools.partial(_inproj_kernel, col_chunk=512),
        grid=(B, S // tm),
        in_specs=[row(D), vec(D), full(wf), full(wq), full(wk), full(wv), full(wm),
                  vec(HEAD_DIM), vec(HEAD_DIM), vec(HEAD_DIM), tab, tab],
        out_specs=[row(FW), row(AW), row(AW), row(AW), row(MW)],
        out_shape=[jax.ShapeDtypeStruct((B, S, FW), F32),
                   jax.ShapeDtypeStruct((B, S, AW), BF16),
                   jax.ShapeDtypeStruct((B, S, AW), BF16),
                   jax.ShapeDtypeStruct((B, S, AW), BF16),
                   jax.ShapeDtypeStruct((B, S, MW), BF16)],
        compiler_params=_params("parallel", "parallel"),
        name="inproj",
    )(x, g.reshape(1, D), wf, wq, wk, wv, wm, gq.reshape(1, -1), gk.reshape(1, -1), gmq.reshape(1, -1),
      cos2, sin2)


def _fft1_kernel(m_ref, x_ref, o_ref):
    o_ref[...] = _dot(m_ref[...], x_ref[...].astype(BF16))


def _fft1(f_in, m1, *, tn=8192):
    B, S, C = f_in.shape
    N1 = m1.shape[1]
    N2 = S // N1
    W = N2 * C
    tn = min(tn, W)
    assert W % tn == 0
    x = f_in.reshape(B, N1, W)
    return pl.pallas_call(
        _fft1_kernel,
        grid=(B, W // tn),
        in_specs=[pl.BlockSpec((2 * N1, N1), lambda b, j: (0, 0)),
                  pl.BlockSpec((None, N1, tn), lambda b, j: (b, 0, j))],
        out_specs=pl.BlockSpec((None, 2 * N1, tn), lambda b, j: (b, 0, j)),
        out_shape=jax.ShapeDtypeStruct((B, 2 * N1, W), F32),
        compiler_params=_params("parallel", "parallel"),
        name="fft1",
    )(m1, x)


def _fft2_kernel(y_ref, tc_ref, ts_ref, m2_ref, cbd_ref, sbd_ref, wf_ref, go_ref, o_ref, *, G, N2, C, scale):
    reps = C // HEAD_DIM
    for g in range(G):
        rows = slice(g * N2, (g + 1) * N2)
        br = y_ref[0, rows, :]
        bi = y_ref[1, rows, :]
        tc = jnp.concatenate([tc_ref[rows, :]] * reps, axis=1)
        ts = jnp.concatenate([ts_ref[rows, :]] * reps, axis=1)
        pr = br * tc + bi * ts
        pi = bi * tc - br * ts
        z = _dot(m2_ref[...], jnp.concatenate([pr, pi], axis=0).astype(BF16))
        y = _dot(z[:N2].astype(BF16), cbd_ref[...]) + _dot(z[N2:].astype(BF16), sbd_ref[...])
        yf = _dot((y * scale).astype(BF16), wf_ref[...])
        o_ref[:, g * C:(g + 1) * C] = _rms(yf, go_ref[...]).astype(BF16)


def _fft2(y1, tc, ts, m2, cbd, sbd, wf, go, *, S, G=8):
    B = y1.shape[0]
    C = wf.shape[0]
    N2 = m2.shape[0] // 2
    N1 = S // N2
    assert N1 % G == 0
    y = y1.reshape(B, 2, S, C)
    full = lambda a: pl.BlockSpec(a.shape, lambda b, kb: (0,) * a.ndim)
    scale = 1.0 / math.sqrt(S * HEAD_DIM)
    out = pl.pallas_call(
        functools.partial(_fft2_kernel, G=G, N2=N2, C=C, scale=scale),
        grid=(B, N1 // G),
        in_specs=[pl.BlockSpec((None, 2, G * N2, C), lambda b, kb: (b, 0, kb, 0)),
                  pl.BlockSpec((G * N2, HEAD_DIM), lambda b, kb: (kb, 0)),
                  pl.BlockSpec((G * N2, HEAD_DIM), lambda b, kb: (kb, 0)),
                  full(m2), full(cbd), full(sbd), full(wf),
                  pl.BlockSpec((1, C), lambda b, kb: (0, 0))],
        out_specs=pl.BlockSpec((None, N2, G * C), lambda b, kb: (b, 0, kb)),
        out_shape=jax.ShapeDtypeStruct((B, N2, N1 * C), BF16),
        compiler_params=_params("parallel", "parallel"),
        name="fft2",
    )(y, tc, ts, m2, cbd, sbd, wf, go.reshape(1, C))
    return out.reshape(B, S, C)


def _dft_tables(S, C):
    N1 = FFT_N1
    N2 = S // N1

    def cs(n):
        j = jnp.arange(n, dtype=jnp.int32)
        ang = ((j[:, None] * j[None, :]) % n).astype(F32) * (2.0 * math.pi / n)
        return jnp.cos(ang), jnp.sin(ang)

    c1, s1 = cs(N1)
    m1 = jnp.concatenate([c1, -s1], axis=0).astype(BF16)
    c2, s2 = cs(N2)
    m2 = jnp.concatenate([jnp.concatenate([c2, s2], axis=1),
                          jnp.concatenate([-s2, c2], axis=1)], axis=0).astype(BF16)
    cc, sc = cs(HEAD_DIM)
    eye = jnp.eye(C // HEAD_DIM, dtype=F32)
    cbd = jnp.kron(eye, cc).astype(BF16)
    sbd = jnp.kron(eye, sc).astype(BF16)
    k1 = jnp.arange(N1, dtype=jnp.int32)[:, None]
    n2 = jnp.arange(N2, dtype=jnp.int32)[None, :]
    ang = (k1 * n2).astype(F32) * (2.0 * math.pi / S)
    tc = jnp.broadcast_to(jnp.cos(ang).reshape(S, 1), (S, HEAD_DIM))
    ts = jnp.broadcast_to(jnp.sin(ang).reshape(S, 1), (S, HEAD_DIM))
    return m1, m2, cbd, sbd, tc, ts


def _band_kernel(q_ref, k_ref, v_ref, o_ref, lse_ref, *, L, heads):
    n = pl.program_id(2)
    ws = pl.multiple_of(jnp.clip(n * Q_BLOCK - HALF_WINDOW, 0, L - K_WINDOW), HALF_WINDOW)
    rows = n * Q_BLOCK + lax.broadcasted_iota(jnp.int32, (Q_BLOCK, K_WINDOW), 0)
    cols = ws + lax.broadcasted_iota(jnp.int32, (Q_BLOCK, K_WINDOW), 1)
    valid = jnp.abs(rows - cols) <= HALF_WINDOW
    lane = lax.broadcasted_iota(jnp.int32, (Q_BLOCK, HEAD_DIM), 1)
    lse_mat = jnp.zeros((Q_BLOCK, HEAD_DIM), F32)
    scale = HEAD_DIM ** -0.5
    for h in range(heads):
        sl = slice(h * HEAD_DIM, (h + 1) * HEAD_DIM)
        q = q_ref[:, sl]
        k = k_ref[pl.ds(ws, K_WINDOW), sl]
        v = v_ref[pl.ds(ws, K_WINDOW), sl]
        s = lax.dot_general(q, k, (((1,), (1,)), ((), ())), preferred_element_type=F32) * scale
        s = jnp.where(valid, s, NEG_BIG)
        m = jnp.max(s, axis=-1, keepdims=True)
        p = jnp.exp(s - m)
        l = jnp.sum(p, axis=-1, keepdims=True)
        o_ref[:, sl] = _dot(p.astype(BF16), v) * (1.0 / l)
        lse_mat = jnp.where(lane == h, m + jnp.log(l), lse_mat)
    lse_ref[...] = lse_mat


def _band_attention(q, k, v):
    G, L, AW = q.shape
    assert L % Q_BLOCK == 0 and L >= K_WINDOW
    hw = AW
    while L * hw > 2 * 1024 * 1024 and hw > HEAD_DIM:
        hw //= 2
    nhg = AW // hw
    qo = pl.BlockSpec((None, Q_BLOCK, hw), lambda g, hg, n: (g, n, hg))
    kv = pl.BlockSpec((None, L, hw), lambda g, hg, n: (g, 0, hg))
    return pl.pallas_call(
        functools.partial(_band_kernel, L=L, heads=hw // HEAD_DIM),
        grid=(G, nhg, L // Q_BLOCK),
        in_specs=[qo, kv, kv],
        out_specs=[qo, pl.BlockSpec((None, Q_BLOCK, HEAD_DIM), lambda g, hg, n: (g, n, hg))],
        out_shape=[jax.ShapeDtypeStruct((G, L, AW), F32),
                   jax.ShapeDtypeStruct((G, L, HEAD_DIM * nhg), F32)],
        compiler_params=_params("parallel", "parallel", "arbitrary"),
        name="band_attention",
    )(q, k, v)


def _to_classes(t, dil):
    B, S, W = t.shape
    if dil == 1:
        return t
    return t.reshape(B, S // dil, dil, W).transpose(0, 2, 1, 3).reshape(B * dil, S // dil, W)


def _from_classes(t, dil, B):
    G, L, W = t.shape
    if dil == 1:
        return t
    return t.reshape(B, dil, L, W).transpose(0, 2, 1, 3).reshape(B, L * dil, W)


def _memkv_kernel(m_ref, g_ref, w_ref, gk_ref, k_ref, v_ref, *, MW):
    h = _rms(m_ref[...], g_ref[...]).astype(BF16)
    kv = _dot(h, w_ref[...])
    for j in range(MW // HEAD_DIM):
        sl = slice(j * HEAD_DIM, (j + 1) * HEAD_DIM)
        k_ref[:, sl] = _rms(kv[:, sl], gk_ref[...]).astype(BF16)
    v_ref[...] = kv[:, MW:].astype(BF16)


def _memkv(mem, g, w, gk, *, tm=256):
    B, M, D = mem.shape
    MW = w.shape[1] // 2
    rows = B * M
    tm = min(tm, rows)
    assert rows % tm == 0
    out = pl.BlockSpec((tm, MW), lambda i: (i, 0))
    return pl.pallas_call(
        functools.partial(_memkv_kernel, MW=MW),
        grid=(rows // tm,),
        in_specs=[pl.BlockSpec((tm, D), lambda i: (i, 0)),
                  pl.BlockSpec((1, D), lambda i: (0, 0)),
                  pl.BlockSpec(w.shape, lambda i: (0, 0)),
                  pl.BlockSpec((1, HEAD_DIM), lambda i: (0, 0))],
        out_specs=[out, out],
        out_shape=[jax.ShapeDtypeStruct((rows, MW), BF16)] * 2,
        compiler_params=_params("parallel"),
        name="memkv",
    )(mem.reshape(rows, D), g.reshape(1, D), w, gk.reshape(1, -1))


def _memattn_kernel(q_ref, k_ref, v_ref, go_ref, o_ref, y_ref, *, MW):
    scale = HEAD_DIM ** -0.5
    for j in range(MW // HEAD_DIM):
        sl = slice(j * HEAD_DIM, (j + 1) * HEAD_DIM)
        s = lax.dot_general(q_ref[:, sl], k_ref[:, sl], (((1,), (1,)), ((), ())),
                            preferred_element_type=F32) * scale
        m = jnp.max(s, axis=-1, keepdims=True)
        p = jnp.exp(s - m)
        l = jnp.sum(p, axis=-1, keepdims=True)
        y_ref[:, sl] = _dot(p.astype(BF16), v_ref[:, sl]) * (1.0 / l)
    o_ref[...] = _rms(y_ref[...], go_ref[...]).astype(BF16)


def _memattn(mq, mk, mv, go, *, tm=512):
    B, S, MW = mq.shape
    M = mk.shape[1]
    assert S % tm == 0
    row = pl.BlockSpec((None, tm, MW), lambda b, i: (b, i, 0))
    mem = pl.BlockSpec((None, M, MW), lambda b, i: (b, 0, 0))
    return pl.pallas_call(
        functools.partial(_memattn_kernel, MW=MW),
        grid=(B, S // tm),
        in_specs=[row, mem, mem, pl.BlockSpec((1, MW), lambda b, i: (0, 0))],
        out_specs=row,
        out_shape=jax.ShapeDtypeStruct((B, S, MW), BF16),
        scratch_shapes=[pltpu.VMEM((tm, MW), F32)],
        compiler_params=_params("parallel", "parallel"),
        name="memattn",
    )(mq, mk, mv, go.reshape(1, MW))


def _outproj_kernel(x_ref, yf_ref, o1_ref, o2_ref, o3_ref, l1_ref, l2_ref, l3_ref, ym_ref, go_ref,
                    wf_ref, wa_ref, wm_ref, out_ref, ya_ref, *, AW, heads_per_group):
    ss = jnp.zeros((x_ref.shape[0], 1), F32)
    for h in range(AW // HEAD_DIM):
        sl = slice(h * HEAD_DIM, (h + 1) * HEAD_DIM)
        cols = [(h // hpg) * HEAD_DIM + h % hpg for hpg in heads_per_group]
        l1, l2, l3 = (ref[:, c:c + 1] for ref, c in zip((l1_ref, l2_ref, l3_ref), cols))
        top = jnp.maximum(jnp.maximum(l1, l2), l3)
        e1, e2, e3 = jnp.exp(l1 - top), jnp.exp(l2 - top), jnp.exp(l3 - top)
        inv = 1.0 / (e1 + e2 + e3)
        y = (e1 * inv) * o1_ref[:, sl] + (e2 * inv) * o2_ref[:, sl] + (e3 * inv) * o3_ref[:, sl]
        ya_ref[:, sl] = y
        ss = ss + jnp.sum(y * y, axis=-1, keepdims=True)
    ya = (ya_ref[...] * lax.rsqrt(ss * (1.0 / AW) + EPS) * go_ref[...]).astype(BF16)
    out_ref[...] = (x_ref[...] + _dot(yf_ref[...], wf_ref[...]) + _dot(ya, wa_ref[...])
                    + _dot(ym_ref[...], wm_ref[...]))


def _outproj(x, yf, o, lse, ym, go_a, wo_f, wo_a, wo_m, *, tm=256):
    B, S, D = x.shape
    FW, AW, MW = yf.shape[2], o[0].shape[2], ym.shape[2]
    LW = [t.shape[2] for t in lse]
    hpg = tuple(AW // HEAD_DIM // (w // HEAD_DIM) for w in LW)
    assert S % tm == 0
    row = lambda w: pl.BlockSpec((None, tm, w), lambda b, i: (b, i, 0))
    full = lambda a: pl.BlockSpec(a.shape, lambda b, i: (0,) * a.ndim)
    return pl.pallas_call(
        functools.partial(_outproj_kernel, AW=AW, heads_per_group=hpg),
        grid=(B, S // tm),
        in_specs=[row(D), row(FW), row(AW), row(AW), row(AW), row(LW[0]), row(LW[1]), row(LW[2]), row(MW),
                  pl.BlockSpec((1, AW), lambda b, i: (0, 0)), full(wo_f), full(wo_a), full(wo_m)],
        out_specs=row(D),
        out_shape=jax.ShapeDtypeStruct((B, S, D), F32),
        scratch_shapes=[pltpu.VMEM((tm, AW), F32)],
        compiler_params=_params("parallel", "parallel"),
        name="outproj",
    )(x, yf, *o, *lse, ym, go_a.reshape(1, AW), wo_f, wo_a, wo_m)


def _rope_tables(S):
    inv = ROPE_THETA ** (-jnp.arange(0, HEAD_DIM, 2, dtype=F32) / HEAD_DIM)
    ang = jnp.arange(S, dtype=F32)[:, None] * inv[None, :]
    cos, sin = jnp.cos(ang), jnp.sin(ang)
    return jnp.concatenate([cos, cos], axis=1), jnp.concatenate([-sin, sin], axis=1)


def _trunk(x, mem, layers):
    B, S, D = x.shape
    cos2, sin2 = _rope_tables(S)
    m1, m2, cbd, sbd, tc, ts = _dft_tables(S, layers[0]["w_fourier"].shape[0])
    for p in layers:
        x = _ffn(x.reshape(B * S, D), p["g_ffn1"], *p["ffn1"]).reshape(B, S, D)

        f_in, q, k, v, mq = _inproj(x, p["g_mix"], *p["w_in"], p["g_q"], p["g_k"], p["g_mq"], cos2, sin2)

        y_f = _fft2(_fft1(f_in, m1), tc, ts, m2, cbd, sbd, p["w_fourier"], p["g_out"][0], S=S)

        outs, lses = [], []
        for dil in DILATIONS:
            o, lse = _band_attention(_to_classes(q, dil), _to_classes(k, dil), _to_classes(v, dil))
            outs.append(_from_classes(o, dil, B))
            lses.append(_from_classes(lse, dil, B))

        mk, mv = _memkv(mem, p["g_mem"], p["w_mem_kv"], p["g_mk"])
        M = mem.shape[1]
        y_m = _memattn(mq, mk.reshape(B, M, -1), mv.reshape(B, M, -1), p["g_out"][2])

        x = _outproj(x, y_f, outs, lses, y_m, p["g_out"][1], *p["w_out"])
        x = _ffn(x.reshape(B * S, D), p["g_ffn2"], *p["ffn2"]).reshape(B, S, D)
    return x


def kernel(x_prompt, x_sample, mem_prompt, mem_sample, g_ffn1, w_ffn1_gate, w_ffn1_up, w_ffn1_down, g_mix, w_in, g_q, g_k, g_mem, w_mem_kv, g_mq, g_mk, w_fourier, g_out, w_out, g_ffn2, w_ffn2_gate, w_ffn2_up, w_ffn2_down):
    depth = w_in.shape[0]
    FW = w_fourier.shape[1]
    MW = w_mem_kv.shape[2] // 2
    AW = (w_in.shape[2] - FW - MW) // 3
    b16 = lambda a: a.astype(BF16)
    layers = []
    for l in range(depth):
        wi = b16(w_in[l])
        wo = b16(w_out[l])
        go = g_out[l]
        layers.append(dict(
            g_ffn1=g_ffn1[l], ffn1=(b16(w_ffn1_gate[l]), b16(w_ffn1_up[l]), b16(w_ffn1_down[l])),
            g_ffn2=g_ffn2[l], ffn2=(b16(w_ffn2_gate[l]), b16(w_ffn2_up[l]), b16(w_ffn2_down[l])),
            g_mix=g_mix[l],
            w_in=(wi[:, :FW], wi[:, FW:FW + AW], wi[:, FW + AW:FW + 2 * AW], wi[:, FW + 2 * AW:FW + 3 * AW],
                  wi[:, FW + 3 * AW:]),
            g_q=g_q[l], g_k=g_k[l], g_mq=g_mq[l], g_mk=g_mk[l], g_mem=g_mem[l],
            w_mem_kv=b16(w_mem_kv[l]), w_fourier=b16(w_fourier[l]),
            g_out=(go[:FW], go[FW:FW + AW], go[FW + AW:]),
            w_out=(wo[:FW], wo[FW:FW + AW], wo[FW + AW:]),
        ))
    return (_trunk(x_prompt, mem_prompt, layers), _trunk(x_sample, mem_sample, layers))
```

```python
import functools
import math

import jax
import jax.numpy as jnp
from jax import lax
from jax.experimental import pallas as pl
from jax.experimental.pallas import tpu as pltpu

EPS = 1e-6
HEAD_DIM = 128
ROPE_THETA = 10000.0
DILATIONS = (1, 4, 16)
HALF_WINDOW = 64
Q_BLOCK = 2 * HALF_WINDOW
K_WINDOW = 2 * Q_BLOCK
FFT_N1 = 64
NEG_BIG = -1e30
Q_SCALE = HEAD_DIM ** -0.5 * math.log2(math.e)
HEAD_BLOCKS_PER_STEP = 16

BF16 = jnp.bfloat16
F32 = jnp.float32

VMEM_LIMIT_BYTES = 56 * 1024 * 1024


def _params(*semantics):
    return pltpu.CompilerParams(dimension_semantics=semantics, vmem_limit_bytes=VMEM_LIMIT_BYTES)


def _rms(x, g):
    return x * lax.rsqrt(jnp.mean(x * x, axis=-1, keepdims=True) + EPS) * g


def _dot(a, b):
    return jnp.dot(a, b, preferred_element_type=F32)


def _ffn_kernel(x_ref, g_ref, wg_ref, wu_ref, wd_ref, o_ref, h_ref):
    f = pl.program_id(1)

    @pl.when(f == 0)
    def _():
        x = x_ref[...]
        h_ref[...] = _rms(x, g_ref[...]).astype(BF16)
        o_ref[...] = x

    h = h_ref[...]
    gate = _dot(h, wg_ref[...])
    up = _dot(h, wu_ref[...])
    a = (0.5 * gate * (1.0 / (1.0 + jnp.exp(-gate))) * up).astype(BF16)
    o_ref[...] += _dot(a, wd_ref[...])


def _ffn(x, g, wg, wu, wd, *, tm=1024, tf=512):
    T, D = x.shape
    F = wg.shape[1]
    tf = min(tf, F)
    assert T % tm == 0 and F % tf == 0
    return pl.pallas_call(
        _ffn_kernel,
        grid=(T // tm, F // tf),
        in_specs=[
            pl.BlockSpec((tm, D), lambda i, f: (i, 0)),
            pl.BlockSpec((1, D), lambda i, f: (0, 0)),
            pl.BlockSpec((D, tf), lambda i, f: (0, f)),
            pl.BlockSpec((D, tf), lambda i, f: (0, f)),
            pl.BlockSpec((tf, D), lambda i, f: (f, 0)),
        ],
        out_specs=pl.BlockSpec((tm, D), lambda i, f: (i, 0)),
        out_shape=jax.ShapeDtypeStruct((T, D), F32),
        scratch_shapes=[pltpu.VMEM((tm, D), BF16)],
        compiler_params=_params("parallel", "arbitrary"),
        name="ffn",
    )(x, g.reshape(1, D), wg, wu, wd)


def _inproj_kernel(x_ref, g_ref, wf_ref, wq_ref, wk_ref, wv_ref, wm_ref, gq_ref, gk_ref, gmq_ref,
                   cos_ref, sin_ref, f_ref, q1_ref, k1_ref, v1_ref, q4_ref, k4_ref, v4_ref,
                   q16_ref, k16_ref, v16_ref, mq_ref, nat_ref, cm4_ref, *, tm, col_chunk):
    h = _rms(x_ref[...], g_ref[...]).astype(BF16)
    cos = cos_ref[...]
    sin = sin_ref[...]

    f_ref[...] = _dot(h, wf_ref[...])

    def heads(w_ref, emit, fn):
        width = w_ref.shape[1]
        cc = min(col_chunk, width)
        for c0 in range(0, width, cc):
            p = _dot(h, w_ref[:, c0:c0 + cc])
            for j in range(cc // HEAD_DIM):
                emit(c0 // HEAD_DIM + j, fn(p[:, j * HEAD_DIM:(j + 1) * HEAD_DIM]))

    def class_copies(base, out1_ref, out4_ref, out16_ref):
        def emit(head, t):
            sl = slice(head * HEAD_DIM, (head + 1) * HEAD_DIM)
            slab = base + head
            out1_ref[:, sl] = t.astype(BF16)
            nat_ref[slab] = t
            for r4 in range(4):
                c4 = nat_ref[slab, pl.ds(r4, tm // 4, stride=4), :]
                out4_ref[r4, :, sl] = c4.astype(BF16)
                cm4_ref[slab, r4 * (tm // 4):(r4 + 1) * (tm // 4), :] = c4
            for r16 in range(16):
                start = (r16 % 4) * (tm // 4) + r16 // 4
                c16 = cm4_ref[slab, pl.ds(start, tm // 16, stride=4), :]
                out16_ref[r16, :, sl] = c16.astype(BF16)
        return emit

    def norm_rope(g_ref_, post_scale):
        def fn(t):
            t = _rms(t, g_ref_[...])
            t = t * cos + pltpu.roll(t, HEAD_DIM // 2, 1) * sin
            return t if post_scale is None else t * post_scale
        return fn

    def emit_mq(head, t):
        mq_ref[:, head * HEAD_DIM:(head + 1) * HEAD_DIM] = t.astype(BF16)

    nh = wq_ref.shape[1] // HEAD_DIM
    heads(wq_ref, class_copies(0, q1_ref, q4_ref, q16_ref), norm_rope(gq_ref, Q_SCALE))
    heads(wk_ref, class_copies(nh, k1_ref, k4_ref, k16_ref), norm_rope(gk_ref, None))
    heads(wv_ref, class_copies(2 * nh, v1_ref, v4_ref, v16_ref), lambda t: t)
    heads(wm_ref, emit_mq, lambda t: _rms(t, gmq_ref[...]))


def _inproj(x, g, wf, wq, wk, wv, wm, gq, gk, gmq, cos2, sin2, *, tm=256):
    B, S, D = x.shape
    FW, AW, MW = wf.shape[1], wq.shape[1], wm.shape[1]
    assert S % tm == 0 and tm % 256 == 0
    row = lambda w: pl.BlockSpec((None, tm, w), lambda b, i: (b, i, 0))
    cls = lambda d: pl.BlockSpec((None, d, tm // d, AW), lambda b, i: (b, 0, i, 0))
    full = lambda a: pl.BlockSpec(a.shape, lambda b, i: (0,) * a.ndim)
    vec = lambda n: pl.BlockSpec((1, n), lambda b, i: (0, 0))
    tab = pl.BlockSpec((tm, HEAD_DIM), lambda b, i: (i, 0))
    nat = jax.ShapeDtypeStruct((B, S, AW), BF16)
    c4 = jax.ShapeDtypeStruct((B, 4, S // 4, AW), BF16)
    c16 = jax.ShapeDtypeStruct((B, 16, S // 16, AW), BF16)
    slabs = 3 * AW // HEAD_DIM
    return pl.pallas_call(
        functools.partial(_inproj_kernel, tm=tm, col_chunk=512),
        grid=(B, S // tm),
        in_specs=[row(D), vec(D), full(wf), full(wq), full(wk), full(wv), full(wm),
                  vec(HEAD_DIM), vec(HEAD_DIM), vec(HEAD_DIM), tab, tab],
        out_specs=[row(FW), row(AW), row(AW), row(AW), cls(4), cls(4), cls(4), cls(16), cls(16), cls(16),
                   row(MW)],
        out_shape=[jax.ShapeDtypeStruct((B, S, FW), F32), nat, nat, nat, c4, c4, c4, c16, c16, c16,
                   jax.ShapeDtypeStruct((B, S, MW), BF16)],
        scratch_shapes=[pltpu.VMEM((slabs, tm, HEAD_DIM), F32), pltpu.VMEM((slabs, tm, HEAD_DIM), F32)],
        compiler_params=_params("parallel", "parallel"),
        name="inproj",
    )(x, g.reshape(1, D), wf, wq, wk, wv, wm, gq.reshape(1, -1), gk.reshape(1, -1), gmq.reshape(1, -1),
      cos2, sin2)


def _fft1_kernel(m_ref, x_ref, o_ref):
    o_ref[...] = _dot(m_ref[...], x_ref[...].astype(BF16))


def _fft1(f_in, m1, *, tn=8192):
    B, S, C = f_in.shape
    N1 = m1.shape[1]
    N2 = S // N1
    W = N2 * C
    tn = min(tn, W)
    assert W % tn == 0
    x = f_in.reshape(B, N1, W)
    return pl.pallas_call(
        _fft1_kernel,
        grid=(B, W // tn),
        in_specs=[pl.BlockSpec((2 * N1, N1), lambda b, j: (0, 0)),
                  pl.BlockSpec((None, N1, tn), lambda b, j: (b, 0, j))],
        out_specs=pl.BlockSpec((None, 2 * N1, tn), lambda b, j: (b, 0, j)),
        out_shape=jax.ShapeDtypeStruct((B, 2 * N1, W), F32),
        compiler_params=_params("parallel", "parallel"),
        name="fft1",
    )(m1, x)


def _fft2_kernel(y_ref, tc_ref, ts_ref, m2_ref, cbd_ref, sbd_ref, wf_ref, go_ref, o_ref, *, G, N2, C, scale):
    reps = C // HEAD_DIM
    for g in range(G):
        rows = slice(g * N2, (g + 1) * N2)
        br = y_ref[0, rows, :]
        bi = y_ref[1, rows, :]
        tc = jnp.concatenate([tc_ref[rows, :]] * reps, axis=1)
        ts = jnp.concatenate([ts_ref[rows, :]] * reps, axis=1)
        pr = br * tc + bi * ts
        pi = bi * tc - br * ts
        z = _dot(m2_ref[...], jnp.concatenate([pr, pi], axis=0).astype(BF16))
        y = _dot(z[:N2].astype(BF16), cbd_ref[...]) + _dot(z[N2:].astype(BF16), sbd_ref[...])
        yf = _dot((y * scale).astype(BF16), wf_ref[...])
        o_ref[:, g * C:(g + 1) * C] = _rms(yf, go_ref[...]).astype(BF16)


def _fft2(y1, tc, ts, m2, cbd, sbd, wf, go, *, S, G=8):
    B = y1.shape[0]
    C = wf.shape[0]
    N2 = m2.shape[0] // 2
    N1 = S // N2
    assert N1 % G == 0
    y = y1.reshape(B, 2, S, C)
    full = lambda a: pl.BlockSpec(a.shape, lambda b, kb: (0,) * a.ndim)
    scale = 1.0 / math.sqrt(S * HEAD_DIM)
    out = pl.pallas_call(
        functools.partial(_fft2_kernel, G=G, N2=N2, C=C, scale=scale),
        grid=(B, N1 // G),
        in_specs=[pl.BlockSpec((None, 2, G * N2, C), lambda b, kb: (b, 0, kb, 0)),
                  pl.BlockSpec((G * N2, HEAD_DIM), lambda b, kb: (kb, 0)),
                  pl.BlockSpec((G * N2, HEAD_DIM), lambda b, kb: (kb, 0)),
                  full(m2), full(cbd), full(sbd), full(wf),
                  pl.BlockSpec((1, C), lambda b, kb: (0, 0))],
        out_specs=pl.BlockSpec((None, N2, G * C), lambda b, kb: (b, 0, kb)),
        out_shape=jax.ShapeDtypeStruct((B, N2, N1 * C), BF16),
        compiler_params=_params("parallel", "parallel"),
        name="fft2",
    )(y, tc, ts, m2, cbd, sbd, wf, go.reshape(1, C))
    return out.reshape(B, S, C)


def _dft_tables(S, C):
    N1 = FFT_N1
    N2 = S // N1

    def cs(n):
        j = jnp.arange(n, dtype=jnp.int32)
        ang = ((j[:, None] * j[None, :]) % n).astype(F32) * (2.0 * math.pi / n)
        return jnp.cos(ang), jnp.sin(ang)

    c1, s1 = cs(N1)
    m1 = jnp.concatenate([c1, -s1], axis=0).astype(BF16)
    c2, s2 = cs(N2)
    m2 = jnp.concatenate([jnp.concatenate([c2, s2], axis=1),
                          jnp.concatenate([-s2, c2], axis=1)], axis=0).astype(BF16)
    cc, sc = cs(HEAD_DIM)
    eye = jnp.eye(C // HEAD_DIM, dtype=F32)
    cbd = jnp.kron(eye, cc).astype(BF16)
    sbd = jnp.kron(eye, sc).astype(BF16)
    k1 = jnp.arange(N1, dtype=jnp.int32)[:, None]
    n2 = jnp.arange(N2, dtype=jnp.int32)[None, :]
    ang = (k1 * n2).astype(F32) * (2.0 * math.pi / S)
    tc = jnp.broadcast_to(jnp.cos(ang).reshape(S, 1), (S, HEAD_DIM))
    ts = jnp.broadcast_to(jnp.sin(ang).reshape(S, 1), (S, HEAD_DIM))
    return m1, m2, cbd, sbd, tc, ts


def _band_kernel(q_ref, k_ref, v_ref, o_ref, lse_ref, *, L, heads, qb):
    step = pl.program_id(2)
    ones = jnp.ones((K_WINDOW, HEAD_DIM), BF16)
    lane = lax.broadcasted_iota(jnp.int32, (Q_BLOCK, HEAD_DIM), 1)
    offs = (lax.broadcasted_iota(jnp.int32, (Q_BLOCK, K_WINDOW), 0)
            - lax.broadcasted_iota(jnp.int32, (Q_BLOCK, K_WINDOW), 1))
    for b in range(qb):
        rows = slice(b * Q_BLOCK, (b + 1) * Q_BLOCK)
        q0 = (step * qb + b) * Q_BLOCK
        ws = pl.multiple_of(jnp.clip(q0 - HALF_WINDOW, 0, L - K_WINDOW), HALF_WINDOW)
        bias = jnp.where(jnp.abs(offs + (q0 - ws)) <= HALF_WINDOW, 0.0, NEG_BIG)
        lse_mat = jnp.zeros((Q_BLOCK, HEAD_DIM), F32)
        for h in range(heads):
            sl = slice(h * HEAD_DIM, (h + 1) * HEAD_DIM)
            k = k_ref[pl.ds(ws, K_WINDOW), sl]
            v = v_ref[pl.ds(ws, K_WINDOW), sl]
            s = lax.dot_general(q_ref[rows, sl], k, (((1,), (1,)), ((), ())),
                                preferred_element_type=F32) + bias
            m = jnp.max(s, axis=-1, keepdims=True)
            p = jnp.exp2(s - m).astype(BF16)
            oa = _dot(p, jnp.concatenate([v, ones], axis=1))
            l = oa[:, HEAD_DIM:]
            o_ref[rows, sl] = (oa[:, :HEAD_DIM] * (1.0 / l)).astype(BF16)
            lse_mat = jnp.where(lane == h, m + jnp.log2(l), lse_mat)
        lse_ref[rows, :] = lse_mat


def _band_attention(q, k, v):
    G, L, AW = q.shape
    assert L % Q_BLOCK == 0 and L >= K_WINDOW
    hw = AW
    while L * hw > 2 * 1024 * 1024 and hw > HEAD_DIM:
        hw //= 2
    nhg = AW // hw
    heads = hw // HEAD_DIM
    qb = max(1, min(HEAD_BLOCKS_PER_STEP // heads, L // Q_BLOCK))
    assert (L // Q_BLOCK) % qb == 0
    qo = pl.BlockSpec((None, qb * Q_BLOCK, hw), lambda g, hg, n: (g, n, hg))
    kv = pl.BlockSpec((None, L, hw), lambda g, hg, n: (g, 0, hg))
    return pl.pallas_call(
        functools.partial(_band_kernel, L=L, heads=heads, qb=qb),
        grid=(G, nhg, L // (qb * Q_BLOCK)),
        in_specs=[qo, kv, kv],
        out_specs=[qo, pl.BlockSpec((None, qb * Q_BLOCK, HEAD_DIM), lambda g, hg, n: (g, n, hg))],
        out_shape=[jax.ShapeDtypeStruct((G, L, AW), BF16),
                   jax.ShapeDtypeStruct((G, L, HEAD_DIM * nhg), F32)],
        compiler_params=_params("parallel", "parallel", "arbitrary"),
        name="band_attention",
    )(q, k, v)


def _memkv_kernel(m_ref, g_ref, w_ref, gk_ref, k_ref, v_ref, *, MW):
    h = _rms(m_ref[...], g_ref[...]).astype(BF16)
    kv = _dot(h, w_ref[...])
    for j in range(MW // HEAD_DIM):
        sl = slice(j * HEAD_DIM, (j + 1) * HEAD_DIM)
        k_ref[:, sl] = _rms(kv[:, sl], gk_ref[...]).astype(BF16)
    v_ref[...] = kv[:, MW:].astype(BF16)


def _memkv(mem, g, w, gk, *, tm=256):
    B, M, D = mem.shape
    MW = w.shape[1] // 2
    rows = B * M
    tm = min(tm, rows)
    assert rows % tm == 0
    out = pl.BlockSpec((tm, MW), lambda i: (i, 0))
    return pl.pallas_call(
        functools.partial(_memkv_kernel, MW=MW),
        grid=(rows // tm,),
        in_specs=[pl.BlockSpec((tm, D), lambda i: (i, 0)),
                  pl.BlockSpec((1, D), lambda i: (0, 0)),
                  pl.BlockSpec(w.shape, lambda i: (0, 0)),
                  pl.BlockSpec((1, HEAD_DIM), lambda i: (0, 0))],
        out_specs=[out, out],
        out_shape=[jax.ShapeDtypeStruct((rows, MW), BF16)] * 2,
        compiler_params=_params("parallel"),
        name="memkv",
    )(mem.reshape(rows, D), g.reshape(1, D), w, gk.reshape(1, -1))


def _memattn_kernel(q_ref, k_ref, v_ref, go_ref, o_ref, y_ref, *, MW):
    scale = HEAD_DIM ** -0.5
    for j in range(MW // HEAD_DIM):
        sl = slice(j * HEAD_DIM, (j + 1) * HEAD_DIM)
        s = lax.dot_general(q_ref[:, sl], k_ref[:, sl], (((1,), (1,)), ((), ())),
                            preferred_element_type=F32) * scale
        m = jnp.max(s, axis=-1, keepdims=True)
        p = jnp.exp(s - m)
        l = jnp.sum(p, axis=-1, keepdims=True)
        y_ref[:, sl] = _dot(p.astype(BF16), v_ref[:, sl]) * (1.0 / l)
    o_ref[...] = _rms(y_ref[...], go_ref[...]).astype(BF16)


def _memattn(mq, mk, mv, go, *, tm=512):
    B, S, MW = mq.shape
    M = mk.shape[1]
    assert S % tm == 0
    row = pl.BlockSpec((None, tm, MW), lambda b, i: (b, i, 0))
    mem = pl.BlockSpec((None, M, MW), lambda b, i: (b, 0, 0))
    return pl.pallas_call(
        functools.partial(_memattn_kernel, MW=MW),
        grid=(B, S // tm),
        in_specs=[row, mem, mem, pl.BlockSpec((1, MW), lambda b, i: (0, 0))],
        out_specs=row,
        out_shape=jax.ShapeDtypeStruct((B, S, MW), BF16),
        scratch_shapes=[pltpu.VMEM((tm, MW), F32)],
        compiler_params=_params("parallel", "parallel"),
        name="memattn",
    )(mq, mk, mv, go.reshape(1, MW))


def _outproj_kernel(x_ref, yf_ref, o1_ref, o4_ref, o16_ref, l1_ref, l4_ref, l16_ref, ym_ref, go_ref,
                    p4_ref, p16_ref, wf_ref, wa_ref, wm_ref, out_ref, ya_ref, ln4_ref, ln16_ref,
                    *, tm, AW, heads_per_group):
    o1 = o1_ref[...]
    o4 = _dot(p4_ref[...], o4_ref[...].reshape(tm, AW))
    o16 = _dot(p16_ref[...], o16_ref[...].reshape(tm, AW))
    for dil, src, dst in ((4, l4_ref, ln4_ref), (16, l16_ref, ln16_ref)):
        for g in range(src.shape[-1] // HEAD_DIM):
            for r in range(dil):
                dst[g, pl.ds(r, tm // dil, stride=dil), :] = src[r, :, g * HEAD_DIM:(g + 1) * HEAD_DIM]
    ss = jnp.zeros((tm, 1), F32)
    for h in range(AW // HEAD_DIM):
        sl = slice(h * HEAD_DIM, (h + 1) * HEAD_DIM)
        (g1, c1), (g4, c4), (g16, c16) = [(h // hpg, h % hpg) for hpg in heads_per_group]
        l1 = l1_ref[:, g1 * HEAD_DIM + c1:g1 * HEAD_DIM + c1 + 1]
        l4 = ln4_ref[g4, :, c4:c4 + 1]
        l16 = ln16_ref[g16, :, c16:c16 + 1]
        top = jnp.maximum(jnp.maximum(l1, l4), l16)
        e1, e4, e16 = jnp.exp2(l1 - top), jnp.exp2(l4 - top), jnp.exp2(l16 - top)
        inv = 1.0 / (e1 + e4 + e16)
        y = (e1 * inv) * o1[:, sl] + (e4 * inv) * o4[:, sl] + (e16 * inv) * o16[:, sl]
        ya_ref[:, sl] = y
        ss = ss + jnp.sum(y * y, axis=-1, keepdims=True)
    ya = (ya_ref[...] * lax.rsqrt(ss * (1.0 / AW) + EPS) * go_ref[...]).astype(BF16)
    out_ref[...] = (x_ref[...] + _dot(yf_ref[...], wf_ref[...]) + _dot(ya, wa_ref[...])
                    + _dot(ym_ref[...], wm_ref[...]))


def _unpermute_matrix(tm, dil):
    t = jnp.arange(tm, dtype=jnp.int32)
    return jax.nn.one_hot((t % dil) * (tm // dil) + t // dil, tm, dtype=BF16)


def _outproj(x, yf, o, lse, ym, go_a, wo_f, wo_a, wo_m, *, tm=256):
    B, S, D = x.shape
    FW, AW, MW = yf.shape[-1], o[0].shape[-1], ym.shape[-1]
    LW = [t.shape[-1] for t in lse]
    hpg = tuple(AW // HEAD_DIM // (w // HEAD_DIM) for w in LW)
    assert S % tm == 0 and tm % 256 == 0
    row = lambda w: pl.BlockSpec((None, tm, w), lambda b, i: (b, i, 0))
    cls = lambda d, w: pl.BlockSpec((None, d, tm // d, w), lambda b, i: (b, 0, i, 0))
    full = lambda a: pl.BlockSpec(a.shape, lambda b, i: (0,) * a.ndim)
    p4, p16 = _unpermute_matrix(tm, 4), _unpermute_matrix(tm, 16)
    return pl.pallas_call(
        functools.partial(_outproj_kernel, tm=tm, AW=AW, heads_per_group=hpg),
        grid=(B, S // tm),
        in_specs=[row(D), row(FW), row(AW), cls(4, AW), cls(16, AW), row(LW[0]), cls(4, LW[1]), cls(16, LW[2]),
                  row(MW), pl.BlockSpec((1, AW), lambda b, i: (0, 0)), full(p4), full(p16),
                  full(wo_f), full(wo_a), full(wo_m)],
        out_specs=row(D),
        out_shape=jax.ShapeDtypeStruct((B, S, D), F32),
        scratch_shapes=[pltpu.VMEM((tm, AW), F32),
                        pltpu.VMEM((LW[1] // HEAD_DIM, tm, HEAD_DIM), F32),
                        pltpu.VMEM((LW[2] // HEAD_DIM, tm, HEAD_DIM), F32)],
        compiler_params=_params("parallel", "parallel"),
        name="outproj",
    )(x, yf, *o, *lse, ym, go_a.reshape(1, AW), p4, p16, wo_f, wo_a, wo_m)


def _rope_tables(S):
    inv = ROPE_THETA ** (-jnp.arange(0, HEAD_DIM, 2, dtype=F32) / HEAD_DIM)
    ang = jnp.arange(S, dtype=F32)[:, None] * inv[None, :]
    cos, sin = jnp.cos(ang), jnp.sin(ang)
    return jnp.concatenate([cos, cos], axis=1), jnp.concatenate([-sin, sin], axis=1)


def _trunk(x, mem, layers):
    B, S, D = x.shape
    cos2, sin2 = _rope_tables(S)
    m1, m2, cbd, sbd, tc, ts = _dft_tables(S, layers[0]["w_fourier"].shape[0])
    for p in layers:
        x = _ffn(x.reshape(B * S, D), p["g_ffn1"], *p["ffn1"]).reshape(B, S, D)

        f_in, *qkv, mq = _inproj(x, p["g_mix"], *p["w_in"], p["g_q"], p["g_k"], p["g_mq"], cos2, sin2)

        y_f = _fft2(_fft1(f_in, m1), tc, ts, m2, cbd, sbd, p["w_fourier"], p["g_out"][0], S=S)

        outs, lses = [], []
        for i, dil in enumerate(DILATIONS):
            q, k, v = (t.reshape(B * dil, S // dil, t.shape[-1]) for t in qkv[3 * i:3 * i + 3])
            o, lse = _band_attention(q, k, v)
            lead = (B, S) if dil == 1 else (B, dil, S // dil)
            outs.append(o.reshape(*lead, o.shape[-1]))
            lses.append(lse.reshape(*lead, lse.shape[-1]))

        mk, mv = _memkv(mem, p["g_mem"], p["w_mem_kv"], p["g_mk"])
        M = mem.shape[1]
        y_m = _memattn(mq, mk.reshape(B, M, -1), mv.reshape(B, M, -1), p["g_out"][2])

        x = _outproj(x, y_f, outs, lses, y_m, p["g_out"][1], *p["w_out"])
        x = _ffn(x.reshape(B * S, D), p["g_ffn2"], *p["ffn2"]).reshape(B, S, D)
    return x


def kernel(x_prompt, x_sample, mem_prompt, mem_sample, g_ffn1, w_ffn1_gate, w_ffn1_up, w_ffn1_down, g_mix, w_in, g_q, g_k, g_mem, w_mem_kv, g_mq, g_mk, w_fourier, g_out, w_out, g_ffn2, w_ffn2_gate, w_ffn2_up, w_ffn2_down):
    depth = w_in.shape[0]
    FW = w_fourier.shape[1]
    MW = w_mem_kv.shape[2] // 2
    AW = (w_in.shape[2] - FW - MW) // 3
    b16 = lambda a: a.astype(BF16)
    layers = []
    for l in range(depth):
        wi = b16(w_in[l])
        wo = b16(w_out[l])
        go = g_out[l]
        layers.append(dict(
            g_ffn1=g_ffn1[l], ffn1=(b16(w_ffn1_gate[l]), b16(w_ffn1_up[l]), b16(w_ffn1_down[l])),
            g_ffn2=g_ffn2[l], ffn2=(b16(w_ffn2_gate[l]), b16(w_ffn2_up[l]), b16(w_ffn2_down[l])),
            g_mix=g_mix[l],
            w_in=(wi[:, :FW], wi[:, FW:FW + AW], wi[:, FW + AW:FW + 2 * AW], wi[:, FW + 2 * AW:FW + 3 * AW],
                  wi[:, FW + 3 * AW:]),
            g_q=g_q[l], g_k=g_k[l], g_mq=g_mq[l], g_mk=g_mk[l], g_mem=g_mem[l],
            w_mem_kv=b16(w_mem_kv[l]), w_fourier=b16(w_fourier[l]),
            g_out=(go[:FW], go[FW:FW + AW], go[FW + AW:]),
            w_out=(wo[:FW], wo[FW:FW + AW], wo[FW + AW:]),
        ))
    return (_trunk(x_prompt, mem_prompt, layers), _trunk(x_sample, mem_sample, layers))
```

```python
import functools
import math

import jax
import jax.numpy as jnp
from jax import lax
from jax.experimental import pallas as pl
from jax.experimental.pallas import tpu as pltpu

EPS = 1e-6
HEAD_DIM = 128
ROPE_THETA = 10000.0
DILATIONS = (1, 4, 16)
HALF_WINDOW = 64
Q_BLOCK = 2 * HALF_WINDOW
K_WINDOW = 2 * Q_BLOCK
FFT_N1 = 64
NEG_BIG = -1e30
Q_SCALE = HEAD_DIM ** -0.5 * math.log2(math.e)
HEAD_BLOCKS_PER_STEP = 32

BF16 = jnp.bfloat16
F32 = jnp.float32

VMEM_LIMIT_BYTES = 56 * 1024 * 1024


def _params(*semantics):
    return pltpu.CompilerParams(dimension_semantics=semantics, vmem_limit_bytes=VMEM_LIMIT_BYTES)


def _rms(x, g):
    return x * lax.rsqrt(jnp.mean(x * x, axis=-1, keepdims=True) + EPS) * g


def _dot(a, b):
    return jnp.dot(a, b, preferred_element_type=F32)


def _ffn_kernel(x_ref, g_ref, wg_ref, wu_ref, wd_ref, o_ref, h_ref):
    f = pl.program_id(1)

    @pl.when(f == 0)
    def _():
        x = x_ref[...]
        h_ref[...] = _rms(x, g_ref[...]).astype(BF16)
        o_ref[...] = x

    h = h_ref[...]
    gate = _dot(h, wg_ref[...])
    up = _dot(h, wu_ref[...])
    a = (0.5 * gate * (1.0 / (1.0 + jnp.exp(-gate))) * up).astype(BF16)
    o_ref[...] += _dot(a, wd_ref[...])


def _ffn(x, g, wg, wu, wd, *, tm=1024, tf=512):
    T, D = x.shape
    F = wg.shape[1]
    tf = min(tf, F)
    assert T % tm == 0 and F % tf == 0
    return pl.pallas_call(
        _ffn_kernel,
        grid=(T // tm, F // tf),
        in_specs=[
            pl.BlockSpec((tm, D), lambda i, f: (i, 0)),
            pl.BlockSpec((1, D), lambda i, f: (0, 0)),
            pl.BlockSpec((D, tf), lambda i, f: (0, f)),
            pl.BlockSpec((D, tf), lambda i, f: (0, f)),
            pl.BlockSpec((tf, D), lambda i, f: (f, 0)),
        ],
        out_specs=pl.BlockSpec((tm, D), lambda i, f: (i, 0)),
        out_shape=jax.ShapeDtypeStruct((T, D), F32),
        scratch_shapes=[pltpu.VMEM((tm, D), BF16)],
        compiler_params=_params("parallel", "arbitrary"),
        name="ffn",
    )(x, g.reshape(1, D), wg, wu, wd)


def _inproj_kernel(x_ref, g_ref, wf_ref, wq_ref, wk_ref, wv_ref, wm_ref, gq_ref, gk_ref, gmq_ref,
                   cos_ref, sin_ref, f_ref, q1_ref, k1_ref, v1_ref, q4_ref, k4_ref, v4_ref,
                   q16_ref, k16_ref, v16_ref, mq_ref, nat_ref, cm4_ref, *, tm, col_chunk):
    h = _rms(x_ref[...], g_ref[...]).astype(BF16)
    cos = cos_ref[...]
    sin = sin_ref[...]

    f_ref[...] = _dot(h, wf_ref[...])

    def heads(w_ref, emit, fn):
        width = w_ref.shape[1]
        cc = min(col_chunk, width)
        for c0 in range(0, width, cc):
            p = _dot(h, w_ref[:, c0:c0 + cc])
            for j in range(cc // HEAD_DIM):
                emit(c0 // HEAD_DIM + j, fn(p[:, j * HEAD_DIM:(j + 1) * HEAD_DIM]))

    def class_copies(base, out1_ref, out4_ref, out16_ref):
        def emit(head, t):
            sl = slice(head * HEAD_DIM, (head + 1) * HEAD_DIM)
            slab = base + head
            out1_ref[:, sl] = t.astype(BF16)
            nat_ref[slab] = t
            for r4 in range(4):
                c4 = nat_ref[slab, pl.ds(r4, tm // 4, stride=4), :]
                out4_ref[r4, :, sl] = c4.astype(BF16)
                cm4_ref[slab, r4 * (tm // 4):(r4 + 1) * (tm // 4), :] = c4
            for r16 in range(16):
                start = (r16 % 4) * (tm // 4) + r16 // 4
                c16 = cm4_ref[slab, pl.ds(start, tm // 16, stride=4), :]
                out16_ref[r16, :, sl] = c16.astype(BF16)
        return emit

    def norm_rope(g_ref_, post_scale):
        def fn(t):
            t = _rms(t, g_ref_[...])
            t = t * cos + pltpu.roll(t, HEAD_DIM // 2, 1) * sin
            return t if post_scale is None else t * post_scale
        return fn

    def emit_mq(head, t):
        mq_ref[:, head * HEAD_DIM:(head + 1) * HEAD_DIM] = t.astype(BF16)

    nh = wq_ref.shape[1] // HEAD_DIM
    heads(wq_ref, class_copies(0, q1_ref, q4_ref, q16_ref), norm_rope(gq_ref, Q_SCALE))
    heads(wk_ref, class_copies(nh, k1_ref, k4_ref, k16_ref), norm_rope(gk_ref, None))
    heads(wm_ref, emit_mq, lambda t: _rms(t, gmq_ref[...]))
    heads(wv_ref, class_copies(2 * nh, v1_ref, v4_ref, v16_ref), lambda t: t)


def _inproj(x, g, wf, wq, wk, wv, wm, gq, gk, gmq, cos2, sin2, *, tm=256):
    B, S, D = x.shape
    FW, AW, MW = wf.shape[1], wq.shape[1], wm.shape[1]
    assert S % tm == 0 and tm % 256 == 0
    row = lambda w: pl.BlockSpec((None, tm, w), lambda b, i: (b, i, 0))
    cls = lambda d: pl.BlockSpec((None, d, tm // d, AW), lambda b, i: (b, 0, i, 0))
    full = lambda a: pl.BlockSpec(a.shape, lambda b, i: (0,) * a.ndim)
    vec = lambda n: pl.BlockSpec((1, n), lambda b, i: (0, 0))
    tab = pl.BlockSpec((tm, HEAD_DIM), lambda b, i: (i, 0))
    nat = jax.ShapeDtypeStruct((B, S, AW), BF16)
    c4 = jax.ShapeDtypeStruct((B, 4, S // 4, AW), BF16)
    c16 = jax.ShapeDtypeStruct((B, 16, S // 16, AW), BF16)
    slabs = 3 * AW // HEAD_DIM
    return pl.pallas_call(
        functools.partial(_inproj_kernel, tm=tm, col_chunk=512),
        grid=(B, S // tm),
        in_specs=[row(D), vec(D), full(wf), full(wq), full(wk), full(wv), full(wm),
                  vec(HEAD_DIM), vec(HEAD_DIM), vec(HEAD_DIM), tab, tab],
        out_specs=[row(FW), row(AW), row(AW), row(AW), cls(4), cls(4), cls(4), cls(16), cls(16), cls(16),
                   row(MW)],
        out_shape=[jax.ShapeDtypeStruct((B, S, FW), F32), nat, nat, nat, c4, c4, c4, c16, c16, c16,
                   jax.ShapeDtypeStruct((B, S, MW), BF16)],
        scratch_shapes=[pltpu.VMEM((slabs, tm, HEAD_DIM), F32), pltpu.VMEM((slabs, tm, HEAD_DIM), F32)],
        compiler_params=_params("parallel", "parallel"),
        name="inproj",
    )(x, g.reshape(1, D), wf, wq, wk, wv, wm, gq.reshape(1, -1), gk.reshape(1, -1), gmq.reshape(1, -1),
      cos2, sin2)


def _fft1_kernel(m_ref, x_ref, o_ref):
    o_ref[...] = _dot(m_ref[...], x_ref[...].astype(BF16))


def _fft1(f_in, m1, *, tn=8192):
    B, S, C = f_in.shape
    N1 = m1.shape[1]
    N2 = S // N1
    W = N2 * C
    tn = min(tn, W)
    assert W % tn == 0
    x = f_in.reshape(B, N1, W)
    return pl.pallas_call(
        _fft1_kernel,
        grid=(B, W // tn),
        in_specs=[pl.BlockSpec((2 * N1, N1), lambda b, j: (0, 0)),
                  pl.BlockSpec((None, N1, tn), lambda b, j: (b, 0, j))],
        out_specs=pl.BlockSpec((None, 2 * N1, tn), lambda b, j: (b, 0, j)),
        out_shape=jax.ShapeDtypeStruct((B, 2 * N1, W), F32),
        compiler_params=_params("parallel", "parallel"),
        name="fft1",
    )(m1, x)


def _fft2_kernel(y_ref, tc_ref, ts_ref, m2_ref, cbd_ref, sbd_ref, wf_ref, go_ref, o_ref, *, G, N2, C, scale):
    reps = C // HEAD_DIM
    br, bi = y_ref[0], y_ref[1]
    tc = jnp.concatenate([tc_ref[...]] * reps, axis=1)
    ts = jnp.concatenate([ts_ref[...]] * reps, axis=1)
    pr = (br * tc + bi * ts).astype(BF16)
    pi = (bi * tc - br * ts).astype(BF16)
    blocks = lambda a: [a[g * N2:(g + 1) * N2] for g in range(G)]
    stacked = jnp.concatenate([jnp.concatenate(blocks(pr), axis=1), jnp.concatenate(blocks(pi), axis=1)], axis=0)
    z = _dot(m2_ref[...], stacked)
    rows = lambda a: jnp.concatenate([a[:, g * C:(g + 1) * C] for g in range(G)], axis=0)
    zr, zi = rows(z[:N2]).astype(BF16), rows(z[N2:]).astype(BF16)
    y = _dot(zr, cbd_ref[...]) + _dot(zi, sbd_ref[...])
    yf = _dot((y * scale).astype(BF16), wf_ref[...])
    o = _rms(yf, go_ref[...]).astype(BF16)
    for g in range(G):
        o_ref[:, g * C:(g + 1) * C] = o[g * N2:(g + 1) * N2]


def _fft2(y1, tc, ts, m2, cbd, sbd, wf, go, *, S, G=8):
    B = y1.shape[0]
    C = wf.shape[0]
    N2 = m2.shape[0] // 2
    N1 = S // N2
    assert N1 % G == 0
    y = y1.reshape(B, 2, S, C)
    full = lambda a: pl.BlockSpec(a.shape, lambda b, kb: (0,) * a.ndim)
    scale = 1.0 / math.sqrt(S * HEAD_DIM)
    out = pl.pallas_call(
        functools.partial(_fft2_kernel, G=G, N2=N2, C=C, scale=scale),
        grid=(B, N1 // G),
        in_specs=[pl.BlockSpec((None, 2, G * N2, C), lambda b, kb: (b, 0, kb, 0)),
                  pl.BlockSpec((G * N2, HEAD_DIM), lambda b, kb: (kb, 0)),
                  pl.BlockSpec((G * N2, HEAD_DIM), lambda b, kb: (kb, 0)),
                  full(m2), full(cbd), full(sbd), full(wf),
                  pl.BlockSpec((1, C), lambda b, kb: (0, 0))],
        out_specs=pl.BlockSpec((None, N2, G * C), lambda b, kb: (b, 0, kb)),
        out_shape=jax.ShapeDtypeStruct((B, N2, N1 * C), BF16),
        compiler_params=_params("parallel", "parallel"),
        name="fft2",
    )(y, tc, ts, m2, cbd, sbd, wf, go.reshape(1, C))
    return out.reshape(B, S, C)


def _dft_tables(S, C):
    N1 = FFT_N1
    N2 = S // N1

    def cs(n):
        j = jnp.arange(n, dtype=jnp.int32)
        ang = ((j[:, None] * j[None, :]) % n).astype(F32) * (2.0 * math.pi / n)
        return jnp.cos(ang), jnp.sin(ang)

    c1, s1 = cs(N1)
    m1 = jnp.concatenate([c1, -s1], axis=0).astype(BF16)
    c2, s2 = cs(N2)
    m2 = jnp.concatenate([jnp.concatenate([c2, s2], axis=1),
                          jnp.concatenate([-s2, c2], axis=1)], axis=0).astype(BF16)
    cc, sc = cs(HEAD_DIM)
    eye = jnp.eye(C // HEAD_DIM, dtype=F32)
    cbd = jnp.kron(eye, cc).astype(BF16)
    sbd = jnp.kron(eye, sc).astype(BF16)
    k1 = jnp.arange(N1, dtype=jnp.int32)[:, None]
    n2 = jnp.arange(N2, dtype=jnp.int32)[None, :]
    ang = (k1 * n2).astype(F32) * (2.0 * math.pi / S)
    tc = jnp.broadcast_to(jnp.cos(ang).reshape(S, 1), (S, HEAD_DIM))
    ts = jnp.broadcast_to(jnp.sin(ang).reshape(S, 1), (S, HEAD_DIM))
    return m1, m2, cbd, sbd, tc, ts


def _band_kernel(q_ref, k_ref, v_ref, kp_ref, vp_ref, kn_ref, vn_ref, o_ref, lse_ref, *, L, heads, qb):
    step = pl.program_id(1)
    ones = jnp.ones((K_WINDOW, HEAD_DIM), BF16)
    lane = lax.broadcasted_iota(jnp.int32, (Q_BLOCK, HEAD_DIM), 1)
    col = lax.broadcasted_iota(jnp.int32, (Q_BLOCK, K_WINDOW), 1)
    offs = lax.broadcasted_iota(jnp.int32, (Q_BLOCK, K_WINDOW), 0) - col + HALF_WINDOW
    in_band = jnp.abs(offs) <= HALF_WINDOW

    def window(main_ref, prev_ref, next_ref, b, sl):
        lo, hi = b * Q_BLOCK - HALF_WINDOW, b * Q_BLOCK + K_WINDOW - HALF_WINDOW
        parts = []
        if lo < 0:
            parts.append(prev_ref[HALF_WINDOW:, sl])
        parts.append(main_ref[max(lo, 0):min(hi, qb * Q_BLOCK), sl])
        if hi > qb * Q_BLOCK:
            parts.append(next_ref[:HALF_WINDOW, sl])
        return parts[0] if len(parts) == 1 else jnp.concatenate(parts, axis=0)

    for b in range(qb):
        rows = slice(b * Q_BLOCK, (b + 1) * Q_BLOCK)
        key = (step * qb + b) * Q_BLOCK - HALF_WINDOW + col
        bias = jnp.where(in_band & (key >= 0) & (key < L), 0.0, NEG_BIG)
        lse_mat = jnp.zeros((Q_BLOCK, HEAD_DIM), F32)
        for h in range(heads):
            sl = slice(h * HEAD_DIM, (h + 1) * HEAD_DIM)
            k = window(k_ref, kp_ref, kn_ref, b, sl)
            v = window(v_ref, vp_ref, vn_ref, b, sl)
            s = lax.dot_general(q_ref[rows, sl], k, (((1,), (1,)), ((), ())),
                                preferred_element_type=F32) + bias
            m = jnp.max(s, axis=-1, keepdims=True)
            p = jnp.exp2(s - m).astype(BF16)
            oa = _dot(p, jnp.concatenate([v, ones], axis=1))
            l = oa[:, HEAD_DIM:]
            o_ref[rows, sl] = (oa[:, :HEAD_DIM] * (1.0 / l)).astype(BF16)
            lse_mat = jnp.where(lane == h, m + jnp.log2(l), lse_mat)
        lse_ref[rows, :] = lse_mat


def _band_attention(q, k, v):
    G, L, AW = q.shape
    heads = AW // HEAD_DIM
    nblk = L // Q_BLOCK
    assert L % Q_BLOCK == 0 and heads <= HEAD_DIM
    qb = max(1, min(HEAD_BLOCKS_PER_STEP // heads, nblk))
    assert nblk % qb == 0
    main = pl.BlockSpec((None, qb * Q_BLOCK, AW), lambda g, n: (g, n, 0))
    prev = pl.BlockSpec((None, Q_BLOCK, AW), lambda g, n: (g, jnp.maximum(n * qb - 1, 0), 0))
    nxt = pl.BlockSpec((None, Q_BLOCK, AW), lambda g, n: (g, jnp.minimum((n + 1) * qb, nblk - 1), 0))
    return pl.pallas_call(
        functools.partial(_band_kernel, L=L, heads=heads, qb=qb),
        grid=(G, nblk // qb),
        in_specs=[main, main, main, prev, prev, nxt, nxt],
        out_specs=[main, pl.BlockSpec((None, qb * Q_BLOCK, HEAD_DIM), lambda g, n: (g, n, 0))],
        out_shape=[jax.ShapeDtypeStruct((G, L, AW), BF16),
                   jax.ShapeDtypeStruct((G, L, HEAD_DIM), F32)],
        compiler_params=_params("parallel", "parallel"),
        name="band_attention",
    )(q, k, v, k, v, k, v)


def _memkv_kernel(m_ref, g_ref, w_ref, gk_ref, k_ref, v_ref, *, MW):
    h = _rms(m_ref[...], g_ref[...]).astype(BF16)
    kv = _dot(h, w_ref[...])
    for j in range(MW // HEAD_DIM):
        sl = slice(j * HEAD_DIM, (j + 1) * HEAD_DIM)
        k_ref[:, sl] = _rms(kv[:, sl], gk_ref[...]).astype(BF16)
    v_ref[...] = kv[:, MW:].astype(BF16)


def _memkv(mem, g, w, gk, *, tm=256):
    B, M, D = mem.shape
    MW = w.shape[1] // 2
    rows = B * M
    tm = min(tm, rows)
    assert rows % tm == 0
    out = pl.BlockSpec((tm, MW), lambda i: (i, 0))
    return pl.pallas_call(
        functools.partial(_memkv_kernel, MW=MW),
        grid=(rows // tm,),
        in_specs=[pl.BlockSpec((tm, D), lambda i: (i, 0)),
                  pl.BlockSpec((1, D), lambda i: (0, 0)),
                  pl.BlockSpec(w.shape, lambda i: (0, 0)),
                  pl.BlockSpec((1, HEAD_DIM), lambda i: (0, 0))],
        out_specs=[out, out],
        out_shape=[jax.ShapeDtypeStruct((rows, MW), BF16)] * 2,
        compiler_params=_params("parallel"),
        name="memkv",
    )(mem.reshape(rows, D), g.reshape(1, D), w, gk.reshape(1, -1))


def _memattn_kernel(q_ref, k_ref, v_ref, go_ref, o_ref, y_ref, *, MW):
    scale = HEAD_DIM ** -0.5
    for j in range(MW // HEAD_DIM):
        sl = slice(j * HEAD_DIM, (j + 1) * HEAD_DIM)
        s = lax.dot_general(q_ref[:, sl], k_ref[:, sl], (((1,), (1,)), ((), ())),
                            preferred_element_type=F32) * scale
        m = jnp.max(s, axis=-1, keepdims=True)
        p = jnp.exp(s - m)
        l = jnp.sum(p, axis=-1, keepdims=True)
        y_ref[:, sl] = _dot(p.astype(BF16), v_ref[:, sl]) * (1.0 / l)
    o_ref[...] = _rms(y_ref[...], go_ref[...]).astype(BF16)


def _memattn(mq, mk, mv, go, *, tm=512):
    B, S, MW = mq.shape
    M = mk.shape[1]
    assert S % tm == 0
    row = pl.BlockSpec((None, tm, MW), lambda b, i: (b, i, 0))
    mem = pl.BlockSpec((None, M, MW), lambda b, i: (b, 0, 0))
    return pl.pallas_call(
        functools.partial(_memattn_kernel, MW=MW),
        grid=(B, S // tm),
        in_specs=[row, mem, mem, pl.BlockSpec((1, MW), lambda b, i: (0, 0))],
        out_specs=row,
        out_shape=jax.ShapeDtypeStruct((B, S, MW), BF16),
        scratch_shapes=[pltpu.VMEM((tm, MW), F32)],
        compiler_params=_params("parallel", "parallel"),
        name="memattn",
    )(mq, mk, mv, go.reshape(1, MW))


def _outproj_kernel(x_ref, yf_ref, o1_ref, o4_ref, o16_ref, l1_ref, l4_ref, l16_ref, ym_ref, go_ref,
                    p4_ref, p16_ref, e_ref, wf_ref, wa_ref, wm_ref, out_ref, ln4_ref, ln16_ref, *, tm, AW):
    for dil, src, dst in ((4, l4_ref, ln4_ref), (16, l16_ref, ln16_ref)):
        for r in range(dil):
            dst[pl.ds(r, tm // dil, stride=dil), :] = src[r]
    o1 = o1_ref[...]
    o4 = _dot(p4_ref[...], o4_ref[...].reshape(tm, AW))
    o16 = _dot(p16_ref[...], o16_ref[...].reshape(tm, AW))

    l1, l4, l16 = l1_ref[...], ln4_ref[...], ln16_ref[...]
    top = jnp.maximum(jnp.maximum(l1, l4), l16)
    e1, e4, e16 = jnp.exp2(l1 - top), jnp.exp2(l4 - top), jnp.exp2(l16 - top)
    inv = 1.0 / (e1 + e4 + e16)

    def spread(w):
        hi = w.astype(BF16)
        lo = (w - hi.astype(F32)).astype(BF16)
        return _dot(jnp.concatenate([hi, lo], axis=1), e_ref[...])

    y = spread(e1 * inv) * o1 + spread(e4 * inv) * o4 + spread(e16 * inv) * o16
    ya = _rms(y, go_ref[...]).astype(BF16)
    out_ref[...] = (x_ref[...] + _dot(yf_ref[...], wf_ref[...]) + _dot(ya, wa_ref[...])
                    + _dot(ym_ref[...], wm_ref[...]))


def _unpermute_matrix(tm, dil):
    t = jnp.arange(tm, dtype=jnp.int32)
    return jax.nn.one_hot((t % dil) * (tm // dil) + t // dil, tm, dtype=BF16)


def _outproj(x, yf, o, lse, ym, go_a, wo_f, wo_a, wo_m, *, tm=256):
    B, S, D = x.shape
    FW, AW, MW = yf.shape[-1], o[0].shape[-1], ym.shape[-1]
    assert S % tm == 0 and tm % 256 == 0 and all(t.shape[-1] == HEAD_DIM for t in lse)
    row = lambda w: pl.BlockSpec((None, tm, w), lambda b, i: (b, i, 0))
    cls = lambda d, w: pl.BlockSpec((None, d, tm // d, w), lambda b, i: (b, 0, i, 0))
    full = lambda a: pl.BlockSpec(a.shape, lambda b, i: (0,) * a.ndim)
    p4, p16 = _unpermute_matrix(tm, 4), _unpermute_matrix(tm, 16)
    spread = (jnp.arange(2 * HEAD_DIM, dtype=jnp.int32)[:, None] % HEAD_DIM
              == jnp.arange(AW, dtype=jnp.int32)[None, :] // HEAD_DIM).astype(BF16)
    return pl.pallas_call(
        functools.partial(_outproj_kernel, tm=tm, AW=AW),
        grid=(B, S // tm),
        in_specs=[row(D), row(FW), row(AW), cls(4, AW), cls(16, AW), row(HEAD_DIM), cls(4, HEAD_DIM),
                  cls(16, HEAD_DIM), row(MW), pl.BlockSpec((1, AW), lambda b, i: (0, 0)),
                  full(p4), full(p16), full(spread), full(wo_f), full(wo_a), full(wo_m)],
        out_specs=row(D),
        out_shape=jax.ShapeDtypeStruct((B, S, D), F32),
        scratch_shapes=[pltpu.VMEM((tm, HEAD_DIM), F32), pltpu.VMEM((tm, HEAD_DIM), F32)],
        compiler_params=_params("parallel", "parallel"),
        name="outproj",
    )(x, yf, *o, *lse, ym, go_a.reshape(1, AW), p4, p16, spread, wo_f, wo_a, wo_m)


def _rope_tables(S):
    inv = ROPE_THETA ** (-jnp.arange(0, HEAD_DIM, 2, dtype=F32) / HEAD_DIM)
    ang = jnp.arange(S, dtype=F32)[:, None] * inv[None, :]
    cos, sin = jnp.cos(ang), jnp.sin(ang)
    return jnp.concatenate([cos, cos], axis=1), jnp.concatenate([-sin, sin], axis=1)


def _trunk(x, mem, layers):
    B, S, D = x.shape
    cos2, sin2 = _rope_tables(S)
    m1, m2, cbd, sbd, tc, ts = _dft_tables(S, layers[0]["w_fourier"].shape[0])
    for p in layers:
        x = _ffn(x.reshape(B * S, D), p["g_ffn1"], *p["ffn1"]).reshape(B, S, D)

        f_in, *qkv, mq = _inproj(x, p["g_mix"], *p["w_in"], p["g_q"], p["g_k"], p["g_mq"], cos2, sin2)

        y_f = _fft2(_fft1(f_in, m1), tc, ts, m2, cbd, sbd, p["w_fourier"], p["g_out"][0], S=S)

        outs, lses = [], []
        for i, dil in enumerate(DILATIONS):
            q, k, v = (t.reshape(B * dil, S // dil, t.shape[-1]) for t in qkv[3 * i:3 * i + 3])
            o, lse = _band_attention(q, k, v)
            lead = (B, S) if dil == 1 else (B, dil, S // dil)
            outs.append(o.reshape(*lead, o.shape[-1]))
            lses.append(lse.reshape(*lead, lse.shape[-1]))

        mk, mv = _memkv(mem, p["g_mem"], p["w_mem_kv"], p["g_mk"])
        M = mem.shape[1]
        y_m = _memattn(mq, mk.reshape(B, M, -1), mv.reshape(B, M, -1), p["g_out"][2])

        x = _outproj(x, y_f, outs, lses, y_m, p["g_out"][1], *p["w_out"])
        x = _ffn(x.reshape(B * S, D), p["g_ffn2"], *p["ffn2"]).reshape(B, S, D)
    return x


def kernel(x_prompt, x_sample, mem_prompt, mem_sample, g_ffn1, w_ffn1_gate, w_ffn1_up, w_ffn1_down, g_mix, w_in, g_q, g_k, g_mem, w_mem_kv, g_mq, g_mk, w_fourier, g_out, w_out, g_ffn2, w_ffn2_gate, w_ffn2_up, w_ffn2_down):
    depth = w_in.shape[0]
    FW = w_fourier.shape[1]
    MW = w_mem_kv.shape[2] // 2
    AW = (w_in.shape[2] - FW - MW) // 3
    b16 = lambda a: a.astype(BF16)
    layers = []
    for l in range(depth):
        wi = b16(w_in[l])
        wo = b16(w_out[l])
        go = g_out[l]
        layers.append(dict(
            g_ffn1=g_ffn1[l], ffn1=(b16(w_ffn1_gate[l]), b16(w_ffn1_up[l]), b16(w_ffn1_down[l])),
            g_ffn2=g_ffn2[l], ffn2=(b16(w_ffn2_gate[l]), b16(w_ffn2_up[l]), b16(w_ffn2_down[l])),
            g_mix=g_mix[l],
            w_in=(wi[:, :FW], wi[:, FW:FW + AW], wi[:, FW + AW:FW + 2 * AW], wi[:, FW + 2 * AW:FW + 3 * AW],
                  wi[:, FW + 3 * AW:]),
            g_q=g_q[l], g_k=g_k[l], g_mq=g_mq[l], g_mk=g_mk[l], g_mem=g_mem[l],
            w_mem_kv=b16(w_mem_kv[l]), w_fourier=b16(w_fourier[l]),
            g_out=(go[:FW], go[FW:FW + AW], go[FW + AW:]),
            w_out=(wo[:FW], wo[FW:FW + AW], wo[FW + AW:]),
        ))
    return (_trunk(x_prompt, mem_prompt, layers), _trunk(x_sample, mem_sample, layers))
```

```python
import functools
import math

import jax
import jax.numpy as jnp
from jax import lax
from jax.experimental import pallas as pl
from jax.experimental.pallas import tpu as pltpu

EPS = 1e-6
HEAD_DIM = 128
ROPE_THETA = 10000.0
DILATIONS = (1, 4, 16)
HALF_WINDOW = 64
Q_BLOCK = 2 * HALF_WINDOW
K_WINDOW = 2 * Q_BLOCK
FFT_N2 = 64
FFT_ROWS = 8
NEG_BIG = -1e30
Q_SCALE = HEAD_DIM ** -0.5 * math.log2(math.e)
HEAD_BLOCKS_PER_STEP = 32
INPROJ_SLABS = 8

BF16 = jnp.bfloat16
F32 = jnp.float32

VMEM_LIMIT_BYTES = 56 * 1024 * 1024


def _params(*semantics):
    return pltpu.CompilerParams(dimension_semantics=semantics, vmem_limit_bytes=VMEM_LIMIT_BYTES)


def _rms(x, g):
    return x * lax.rsqrt(jnp.mean(x * x, axis=-1, keepdims=True) + EPS) * g


def _dot(a, b):
    return jnp.dot(a, b, preferred_element_type=F32)


def _ffn_kernel(x_ref, g_ref, wg_ref, wu_ref, wd_ref, o_ref, h_ref):
    f = pl.program_id(1)

    @pl.when(f == 0)
    def _():
        x = x_ref[...]
        h_ref[...] = _rms(x, g_ref[...]).astype(BF16)
        o_ref[...] = x

    h = h_ref[...]
    gate = _dot(h, wg_ref[...])
    up = _dot(h, wu_ref[...])
    a = (0.5 * gate * (1.0 / (1.0 + jnp.exp(-gate))) * up).astype(BF16)
    o_ref[...] += _dot(a, wd_ref[...])


def _ffn(x, g, wg, wu, wd, l, *, tm=1024, tf=512):
    T, D = x.shape
    F = wg.shape[2]
    tf = min(tf, F)
    assert T % tm == 0 and F % tf == 0
    return pl.pallas_call(
        _ffn_kernel,
        grid=(T // tm, F // tf),
        in_specs=[
            pl.BlockSpec((tm, D), lambda i, f: (i, 0)),
            pl.BlockSpec((1, D), lambda i, f: (0, 0)),
            pl.BlockSpec((None, D, tf), lambda i, f: (l, 0, f)),
            pl.BlockSpec((None, D, tf), lambda i, f: (l, 0, f)),
            pl.BlockSpec((None, tf, D), lambda i, f: (l, f, 0)),
        ],
        out_specs=pl.BlockSpec((tm, D), lambda i, f: (i, 0)),
        out_shape=jax.ShapeDtypeStruct((T, D), F32),
        scratch_shapes=[pltpu.VMEM((tm, D), BF16)],
        compiler_params=_params("parallel", "arbitrary"),
        name="ffn",
    )(x, g.reshape(1, D), wg, wu, wd)


def _inproj_kernel(x_ref, g_ref, w_ref, gq_ref, gk_ref, gmq_ref, cos_ref, sin_ref,
                   f_ref, q1_ref, k1_ref, v1_ref, q4_ref, k4_ref, v4_ref, q16_ref, k16_ref, v16_ref, mq_ref,
                   nat_ref, cm4_ref, *, tm, widths, col_chunk):
    FW, AW, MW = widths
    h = _rms(x_ref[...], g_ref[...]).astype(BF16)
    cos = cos_ref[...]
    sin = sin_ref[...]
    slabs = nat_ref.shape[0]

    def heads(col0, width, emit, fn):
        cc = min(col_chunk, width)
        for c0 in range(0, width, cc):
            p = _dot(h, w_ref[:, col0 + c0:col0 + c0 + cc])
            for j in range(cc // HEAD_DIM):
                emit(c0 // HEAD_DIM + j, fn(p[:, j * HEAD_DIM:(j + 1) * HEAD_DIM]))

    def emit_f(head, t):
        f_ref[:, head * HEAD_DIM:(head + 1) * HEAD_DIM] = t

    def class_copies(base, out1_ref, out4_ref, out16_ref):
        def emit(head, t):
            sl = slice(head * HEAD_DIM, (head + 1) * HEAD_DIM)
            slab = (base + head) % slabs
            out1_ref[:, sl] = t.astype(BF16)
            nat_ref[slab] = t
            for r4 in range(4):
                c4 = nat_ref[slab, pl.ds(r4, tm // 4, stride=4), :]
                out4_ref[r4, :, sl] = c4.astype(BF16)
                cm4_ref[slab, r4 * (tm // 4):(r4 + 1) * (tm // 4), :] = c4
            for r16 in range(16):
                start = (r16 % 4) * (tm // 4) + r16 // 4
                c16 = cm4_ref[slab, pl.ds(start, tm // 16, stride=4), :]
                out16_ref[r16, :, sl] = c16.astype(BF16)
        return emit

    def norm_rope(g_ref_, post_scale):
        def fn(t):
            t = _rms(t, g_ref_[...])
            t = t * cos + pltpu.roll(t, HEAD_DIM // 2, 1) * sin
            return t if post_scale is None else t * post_scale
        return fn

    def emit_mq(head, t):
        mq_ref[:, head * HEAD_DIM:(head + 1) * HEAD_DIM] = t.astype(BF16)

    nh = AW // HEAD_DIM
    heads(0, FW, emit_f, lambda t: t)
    heads(FW, AW, class_copies(0, q1_ref, q4_ref, q16_ref), norm_rope(gq_ref, Q_SCALE))
    heads(FW + AW, AW, class_copies(nh, k1_ref, k4_ref, k16_ref), norm_rope(gk_ref, None))
    heads(FW + 3 * AW, MW, emit_mq, lambda t: _rms(t, gmq_ref[...]))
    heads(FW + 2 * AW, AW, class_copies(2 * nh, v1_ref, v4_ref, v16_ref), lambda t: t)


def _inproj(x, g, w, l, widths, gq, gk, gmq, cos2, sin2, *, tm=512):
    B, S, D = x.shape
    FW, AW, MW = widths
    assert S % tm == 0 and tm % 256 == 0
    row = lambda w_: pl.BlockSpec((None, tm, w_), lambda b, i: (b, i, 0))
    cls = lambda d: pl.BlockSpec((None, d, tm // d, AW), lambda b, i: (b, 0, i, 0))
    vec = lambda n: pl.BlockSpec((1, n), lambda b, i: (0, 0))
    tab = pl.BlockSpec((tm, HEAD_DIM), lambda b, i: (i, 0))
    nat = jax.ShapeDtypeStruct((B, S, AW), BF16)
    c4 = jax.ShapeDtypeStruct((B, 4, S // 4, AW), BF16)
    c16 = jax.ShapeDtypeStruct((B, 16, S // 16, AW), BF16)
    return pl.pallas_call(
        functools.partial(_inproj_kernel, tm=tm, widths=widths, col_chunk=512),
        grid=(B, S // tm),
        in_specs=[row(D), vec(D),
                  pl.BlockSpec((None,) + w.shape[1:], lambda b, i: (l, 0, 0), pipeline_mode=pl.Buffered(1)),
                  vec(HEAD_DIM), vec(HEAD_DIM), vec(HEAD_DIM), tab, tab],
        out_specs=[row(FW), row(AW), row(AW), row(AW), cls(4), cls(4), cls(4), cls(16), cls(16), cls(16),
                   row(MW)],
        out_shape=[jax.ShapeDtypeStruct((B, S, FW), F32), nat, nat, nat, c4, c4, c4, c16, c16, c16,
                   jax.ShapeDtypeStruct((B, S, MW), BF16)],
        scratch_shapes=[pltpu.VMEM((INPROJ_SLABS, tm, HEAD_DIM), F32),
                        pltpu.VMEM((INPROJ_SLABS, tm, HEAD_DIM), F32)],
        compiler_params=_params("parallel", "parallel"),
        name="inproj",
    )(x, g.reshape(1, D), w, gq.reshape(1, -1), gk.reshape(1, -1), gmq.reshape(1, -1), cos2, sin2)


def _fft1_kernel(k_ref, x_ref, o_ref):
    n1, r, c = x_ref.shape
    y = _dot(k_ref[...], x_ref[...].reshape(n1 * r, c).astype(BF16))
    o_ref[...] = y.reshape(2, r, n1, c)


def _fft1(f_in, k1r):
    B, S, C = f_in.shape
    N2 = FFT_N2
    N1 = S // N2
    R = FFT_ROWS
    return pl.pallas_call(
        _fft1_kernel,
        grid=(B, N2 // R),
        in_specs=[pl.BlockSpec(k1r.shape, lambda b, j: (0, 0)),
                  pl.BlockSpec((None, N1, R, C), lambda b, j: (b, 0, j, 0))],
        out_specs=pl.BlockSpec((None, 2, R, N1, C), lambda b, j: (b, 0, j, 0, 0)),
        out_shape=jax.ShapeDtypeStruct((B, 2, N2, N1, C), F32),
        compiler_params=_params("parallel", "parallel"),
        name="fft1",
    )(k1r, f_in.reshape(B, N1, N2, C))


def _fft2_kernel(y_ref, tc_ref, ts_ref, k_ref, cbd_ref, sbd_ref, wf_ref, go_ref, o_ref, *, scale):
    _, n2, r, c = y_ref.shape
    rows = n2 * r
    reps = c // HEAD_DIM
    br, bi = y_ref[0].reshape(rows, c), y_ref[1].reshape(rows, c)
    tc = jnp.concatenate([tc_ref[...]] * reps, axis=1)
    ts = jnp.concatenate([ts_ref[...]] * reps, axis=1)
    pr = (br * tc + bi * ts).astype(BF16)
    pi = (bi * tc - br * ts).astype(BF16)
    z = _dot(k_ref[...], jnp.concatenate([pr, pi], axis=0))
    y = _dot(z[:rows].astype(BF16), cbd_ref[...]) + _dot(z[rows:].astype(BF16), sbd_ref[...])
    yf = _dot((y * scale).astype(BF16), wf_ref[...])
    o_ref[...] = _rms(yf, go_ref[...]).reshape(n2, r, c)


def _fft2(y1, tc, ts, k2r, cbd, sbd, wf, l, go):
    B, _, N2, N1, C = y1.shape
    R = FFT_ROWS
    S = N1 * N2
    full = lambda a: pl.BlockSpec(a.shape, lambda b, kb: (0,) * a.ndim)
    tw = pl.BlockSpec((None, N2 * R, HEAD_DIM), lambda b, kb: (kb, 0, 0))
    out = pl.pallas_call(
        functools.partial(_fft2_kernel, scale=1.0 / math.sqrt(S * HEAD_DIM)),
        grid=(B, N1 // R),
        in_specs=[pl.BlockSpec((None, 2, N2, R, C), lambda b, kb: (b, 0, 0, kb, 0)), tw, tw,
                  full(k2r), full(cbd), full(sbd), pl.BlockSpec((None, C, C), lambda b, kb: (l, 0, 0)),
                  pl.BlockSpec((1, C), lambda b, kb: (0, 0))],
        out_specs=pl.BlockSpec((None, N2, R, C), lambda b, kb: (b, 0, kb, 0)),
        out_shape=jax.ShapeDtypeStruct((B, N2, N1, C), F32),
        compiler_params=_params("parallel", "parallel"),
        name="fft2",
    )(y1, tc, ts, k2r, cbd, sbd, wf, go.reshape(1, C))
    return out.reshape(B, S, C)


def _dft_tables(S, C):
    N2 = FFT_N2
    N1 = S // N2
    R = FFT_ROWS

    def cs(n):
        j = jnp.arange(n, dtype=jnp.int32)
        ang = ((j[:, None] * j[None, :]) % n).astype(F32) * (2.0 * math.pi / n)
        return jnp.cos(ang), jnp.sin(ang)

    eye_r = jnp.eye(R, dtype=F32)
    c1, s1 = cs(N1)
    f1 = jnp.stack([c1, -s1])
    k1r = jnp.einsum("akn,jl->ajknl", f1, eye_r).reshape(2 * R * N1, N1 * R).astype(BF16)
    c2, s2 = cs(N2)
    f2 = jnp.stack([jnp.stack([c2, s2], axis=1), jnp.stack([-s2, c2], axis=1)])
    k2r = jnp.einsum("akbn,gh->akgbnh", f2, eye_r).reshape(2 * N2 * R, 2 * N2 * R).astype(BF16)
    cc, sc = cs(HEAD_DIM)
    eye_g = jnp.eye(C // HEAD_DIM, dtype=F32)
    cbd = jnp.kron(eye_g, cc).astype(BF16)
    sbd = jnp.kron(eye_g, sc).astype(BF16)
    k1 = jnp.arange(N1, dtype=jnp.int32).reshape(N1 // R, 1, R)
    n2 = jnp.arange(N2, dtype=jnp.int32).reshape(1, N2, 1)
    ang = (k1 * n2).astype(F32).reshape(N1 // R, N2 * R, 1) * (2.0 * math.pi / S)
    tc = jnp.broadcast_to(jnp.cos(ang), (N1 // R, N2 * R, HEAD_DIM))
    ts = jnp.broadcast_to(jnp.sin(ang), (N1 // R, N2 * R, HEAD_DIM))
    return k1r, k2r, cbd, sbd, tc, ts


def _band_kernel(*refs, L, heads, qb, seqs, halo):
    if halo:
        q_ref, k_ref, v_ref, kp_ref, vp_ref, kn_ref, vn_ref, o_ref, lse_ref = refs
    else:
        q_ref, k_ref, v_ref, o_ref, lse_ref = refs
    step = pl.program_id(1)
    ones = jnp.ones((K_WINDOW, HEAD_DIM), BF16)
    lane = lax.broadcasted_iota(jnp.int32, (Q_BLOCK, HEAD_DIM), 1)
    col = lax.broadcasted_iota(jnp.int32, (Q_BLOCK, K_WINDOW), 1)
    row_minus_col = lax.broadcasted_iota(jnp.int32, (Q_BLOCK, K_WINDOW), 0) - col

    def window(main_ref, prev_ref, next_ref, b, sl):
        lo, hi = b * Q_BLOCK - HALF_WINDOW, b * Q_BLOCK + K_WINDOW - HALF_WINDOW
        parts = []
        if lo < 0:
            parts.append(prev_ref[HALF_WINDOW:, sl])
        parts.append(main_ref[max(lo, 0):min(hi, qb * Q_BLOCK), sl])
        if hi > qb * Q_BLOCK:
            parts.append(next_ref[:HALF_WINDOW, sl])
        return parts[0] if len(parts) == 1 else jnp.concatenate(parts, axis=0)

    for s in range(seqs):
        for b in range(qb):
            rows = slice(b * Q_BLOCK, (b + 1) * Q_BLOCK)
            if halo:
                key = (step * qb + b) * Q_BLOCK - HALF_WINDOW + col
                ok = (jnp.abs(row_minus_col + HALF_WINDOW) <= HALF_WINDOW) & (key >= 0) & (key < L)
                q_s, o_s, lse_s = q_ref, o_ref, lse_ref
            else:
                ws = min(max(b * Q_BLOCK - HALF_WINDOW, 0), L - K_WINDOW)
                ok = jnp.abs(row_minus_col + (b * Q_BLOCK - ws)) <= HALF_WINDOW
                q_s, o_s, lse_s = q_ref.at[s], o_ref.at[s], lse_ref.at[s]
            bias = jnp.where(ok, 0.0, NEG_BIG)
            lse_mat = jnp.zeros((Q_BLOCK, HEAD_DIM), F32)
            for h in range(heads):
                sl = slice(h * HEAD_DIM, (h + 1) * HEAD_DIM)
                if halo:
                    k = window(k_ref, kp_ref, kn_ref, b, sl)
                    v = window(v_ref, vp_ref, vn_ref, b, sl)
                else:
                    k = k_ref[s, ws:ws + K_WINDOW, sl]
                    v = v_ref[s, ws:ws + K_WINDOW, sl]
                sc = lax.dot_general(q_s[rows, sl], k, (((1,), (1,)), ((), ())),
                                     preferred_element_type=F32) + bias
                m = jnp.max(sc, axis=-1, keepdims=True)
                p = jnp.exp2(sc - m).astype(BF16)
                oa = _dot(p, jnp.concatenate([v, ones], axis=1))
                l = oa[:, HEAD_DIM:]
                o_s[rows, sl] = (oa[:, :HEAD_DIM] * (1.0 / l)).astype(BF16)
                lse_mat = jnp.where(lane == h, m + jnp.log2(l), lse_mat)
            lse_s[rows, :] = lse_mat


def _band_attention(q, k, v):
    G, L, AW = q.shape
    heads = AW // HEAD_DIM
    nblk = L // Q_BLOCK
    assert L % Q_BLOCK == 0 and L >= K_WINDOW and heads <= HEAD_DIM
    qb = max(1, min(HEAD_BLOCKS_PER_STEP // heads, nblk))
    assert nblk % qb == 0
    out_shape = [jax.ShapeDtypeStruct((G, L, AW), BF16), jax.ShapeDtypeStruct((G, L, HEAD_DIM), F32)]
    if qb == nblk:
        seqs = max(1, min(HEAD_BLOCKS_PER_STEP // (heads * qb), G))
        assert G % seqs == 0
        blk = lambda w: pl.BlockSpec((seqs, L, w), lambda g, n: (g, 0, 0))
        return pl.pallas_call(
            functools.partial(_band_kernel, L=L, heads=heads, qb=qb, seqs=seqs, halo=False),
            grid=(G // seqs, 1),
            in_specs=[blk(AW)] * 3,
            out_specs=[blk(AW), blk(HEAD_DIM)],
            out_shape=out_shape,
            compiler_params=_params("parallel", "parallel"),
            name="band_attention",
        )(q, k, v)
    main = pl.BlockSpec((None, qb * Q_BLOCK, AW), lambda g, n: (g, n, 0))
    prev = pl.BlockSpec((None, Q_BLOCK, AW), lambda g, n: (g, jnp.maximum(n * qb - 1, 0), 0))
    nxt = pl.BlockSpec((None, Q_BLOCK, AW), lambda g, n: (g, jnp.minimum((n + 1) * qb, nblk - 1), 0))
    return pl.pallas_call(
        functools.partial(_band_kernel, L=L, heads=heads, qb=qb, seqs=1, halo=True),
        grid=(G, nblk // qb),
        in_specs=[main, main, main, prev, prev, nxt, nxt],
        out_specs=[main, pl.BlockSpec((None, qb * Q_BLOCK, HEAD_DIM), lambda g, n: (g, n, 0))],
        out_shape=out_shape,
        compiler_params=_params("parallel", "parallel"),
        name="band_attention",
    )(q, k, v, k, v, k, v)


def _memkv_kernel(m_ref, g_ref, w_ref, gk_ref, k_ref, v_ref, *, MW):
    h = _rms(m_ref[...], g_ref[...]).astype(BF16)
    kv = _dot(h, w_ref[...])
    for j in range(MW // HEAD_DIM):
        sl = slice(j * HEAD_DIM, (j + 1) * HEAD_DIM)
        k_ref[:, sl] = _rms(kv[:, sl], gk_ref[...]).astype(BF16)
    v_ref[...] = kv[:, MW:].astype(BF16)


def _memkv(mem, g, w, l, gk, *, tm=256):
    B, M, D = mem.shape
    MW = w.shape[2] // 2
    rows = B * M
    tm = min(tm, rows)
    assert rows % tm == 0
    out = pl.BlockSpec((tm, MW), lambda i: (i, 0))
    return pl.pallas_call(
        functools.partial(_memkv_kernel, MW=MW),
        grid=(rows // tm,),
        in_specs=[pl.BlockSpec((tm, D), lambda i: (i, 0)),
                  pl.BlockSpec((1, D), lambda i: (0, 0)),
                  pl.BlockSpec((None,) + w.shape[1:], lambda i: (l, 0, 0)),
                  pl.BlockSpec((1, HEAD_DIM), lambda i: (0, 0))],
        out_specs=[out, out],
        out_shape=[jax.ShapeDtypeStruct((rows, MW), BF16)] * 2,
        compiler_params=_params("parallel"),
        name="memkv",
    )(mem.reshape(rows, D), g.reshape(1, D), w, gk.reshape(1, -1))


def _memattn_kernel(q_ref, k_ref, v_ref, go_ref, o_ref, y_ref, *, MW):
    scale = HEAD_DIM ** -0.5
    for j in range(MW // HEAD_DIM):
        sl = slice(j * HEAD_DIM, (j + 1) * HEAD_DIM)
        s = lax.dot_general(q_ref[:, sl], k_ref[:, sl], (((1,), (1,)), ((), ())),
                            preferred_element_type=F32) * scale
        m = jnp.max(s, axis=-1, keepdims=True)
        p = jnp.exp(s - m)
        l = jnp.sum(p, axis=-1, keepdims=True)
        y_ref[:, sl] = _dot(p.astype(BF16), v_ref[:, sl]) * (1.0 / l)
    o_ref[...] = _rms(y_ref[...], go_ref[...]).astype(BF16)


def _memattn(mq, mk, mv, go, *, tm=512):
    B, S, MW = mq.shape
    M = mk.shape[1]
    assert S % tm == 0
    row = pl.BlockSpec((None, tm, MW), lambda b, i: (b, i, 0))
    mem = pl.BlockSpec((None, M, MW), lambda b, i: (b, 0, 0))
    return pl.pallas_call(
        functools.partial(_memattn_kernel, MW=MW),
        grid=(B, S // tm),
        in_specs=[row, mem, mem, pl.BlockSpec((1, MW), lambda b, i: (0, 0))],
        out_specs=row,
        out_shape=jax.ShapeDtypeStruct((B, S, MW), BF16),
        scratch_shapes=[pltpu.VMEM((tm, MW), F32)],
        compiler_params=_params("parallel", "parallel"),
        name="memattn",
    )(mq, mk, mv, go.reshape(1, MW))


def _outproj_kernel(x_ref, yf_ref, o1_ref, o4_ref, o16_ref, l1_ref, l4_ref, l16_ref, ym_ref, go_ref,
                    p4_ref, p16_ref, e_ref, w_ref, out_ref, ln4_ref, ln16_ref, *, tm, AW):
    FW = yf_ref.shape[-1]
    for dil, src, dst in ((4, l4_ref, ln4_ref), (16, l16_ref, ln16_ref)):
        for r in range(dil):
            dst[pl.ds(r, tm // dil, stride=dil), :] = src[r]
    o1 = o1_ref[...]
    o4 = _dot(p4_ref[...], o4_ref[...].reshape(tm, AW))
    o16 = _dot(p16_ref[...], o16_ref[...].reshape(tm, AW))

    l1, l4, l16 = l1_ref[...], ln4_ref[...], ln16_ref[...]
    top = jnp.maximum(jnp.maximum(l1, l4), l16)
    e1, e4, e16 = jnp.exp2(l1 - top), jnp.exp2(l4 - top), jnp.exp2(l16 - top)
    inv = 1.0 / (e1 + e4 + e16)

    def spread(w):
        hi = w.astype(BF16)
        lo = (w - hi.astype(F32)).astype(BF16)
        return _dot(jnp.concatenate([hi, lo], axis=1), e_ref[...])

    y = spread(e1 * inv) * o1 + spread(e4 * inv) * o4 + spread(e16 * inv) * o16
    ya = _rms(y, go_ref[...]).astype(BF16)
    out_ref[...] = (x_ref[...] + _dot(yf_ref[...].astype(BF16), w_ref[:FW]) + _dot(ya, w_ref[FW:FW + AW])
                    + _dot(ym_ref[...], w_ref[FW + AW:]))


def _unpermute_matrix(tm, dil):
    t = jnp.arange(tm, dtype=jnp.int32)
    return jax.nn.one_hot((t % dil) * (tm // dil) + t // dil, tm, dtype=BF16)


def _outproj(x, yf, o, lse, ym, go_a, w, l, *, tm=256):
    B, S, D = x.shape
    FW, AW, MW = yf.shape[-1], o[0].shape[-1], ym.shape[-1]
    assert S % tm == 0 and tm % 256 == 0 and all(t.shape[-1] == HEAD_DIM for t in lse)
    row = lambda w_: pl.BlockSpec((None, tm, w_), lambda b, i: (b, i, 0))
    cls = lambda d, w_: pl.BlockSpec((None, d, tm // d, w_), lambda b, i: (b, 0, i, 0))
    full = lambda a: pl.BlockSpec(a.shape, lambda b, i: (0,) * a.ndim)
    p4, p16 = _unpermute_matrix(tm, 4), _unpermute_matrix(tm, 16)
    spread = (jnp.arange(2 * HEAD_DIM, dtype=jnp.int32)[:, None] % HEAD_DIM
              == jnp.arange(AW, dtype=jnp.int32)[None, :] // HEAD_DIM).astype(BF16)
    return pl.pallas_call(
        functools.partial(_outproj_kernel, tm=tm, AW=AW),
        grid=(B, S // tm),
        in_specs=[row(D), row(FW), row(AW), cls(4, AW), cls(16, AW), row(HEAD_DIM), cls(4, HEAD_DIM),
                  cls(16, HEAD_DIM), row(MW), pl.BlockSpec((1, AW), lambda b, i: (0, 0)),
                  full(p4), full(p16), full(spread),
                  pl.BlockSpec((None,) + w.shape[1:], lambda b, i: (l, 0, 0))],
        out_specs=row(D),
        out_shape=jax.ShapeDtypeStruct((B, S, D), F32),
        scratch_shapes=[pltpu.VMEM((tm, HEAD_DIM), F32), pltpu.VMEM((tm, HEAD_DIM), F32)],
        compiler_params=_params("parallel", "parallel"),
        name="outproj",
    )(x, yf, *o, *lse, ym, go_a.reshape(1, AW), p4, p16, spread, w)


def _rope_tables(S):
    inv = ROPE_THETA ** (-jnp.arange(0, HEAD_DIM, 2, dtype=F32) / HEAD_DIM)
    ang = jnp.arange(S, dtype=F32)[:, None] * inv[None, :]
    cos, sin = jnp.cos(ang), jnp.sin(ang)
    return jnp.concatenate([cos, cos], axis=1), jnp.concatenate([-sin, sin], axis=1)


def _trunk(x, mem, w, g, widths):
    B, S, D = x.shape
    FW, AW, MW = widths
    M = mem.shape[1]
    cos2, sin2 = _rope_tables(S)
    k1r, k2r, cbd, sbd, tc, ts = _dft_tables(S, FW)
    for l in range(w["in"].shape[0]):
        x = _ffn(x.reshape(B * S, D), g["ffn1"][l], *w["ffn1"], l).reshape(B, S, D)

        f_in, *qkv, mq = _inproj(x, g["mix"][l], w["in"], l, widths, g["q"][l], g["k"][l], g["mq"][l],
                                 cos2, sin2)

        g_out = g["out"][l]
        y_f = _fft2(_fft1(f_in, k1r), tc, ts, k2r, cbd, sbd, w["fourier"], l, g_out[:FW])

        outs, lses = [], []
        for i, dil in enumerate(DILATIONS):
            q, k, v = (t.reshape(B * dil, S // dil, AW) for t in qkv[3 * i:3 * i + 3])
            o, lse = _band_attention(q, k, v)
            lead = (B, S) if dil == 1 else (B, dil, S // dil)
            outs.append(o.reshape(*lead, AW))
            lses.append(lse.reshape(*lead, HEAD_DIM))

        mk, mv = _memkv(mem, g["mem"][l], w["mem_kv"], l, g["mk"][l])
        y_m = _memattn(mq, mk.reshape(B, M, MW), mv.reshape(B, M, MW), g_out[FW + AW:])

        x = _outproj(x, y_f, outs, lses, y_m, g_out[FW:FW + AW], w["out"], l)
        x = _ffn(x.reshape(B * S, D), g["ffn2"][l], *w["ffn2"], l).reshape(B, S, D)
    return x


def kernel(x_prompt, x_sample, mem_prompt, mem_sample, g_ffn1, w_ffn1_gate, w_ffn1_up, w_ffn1_down, g_mix, w_in, g_q, g_k, g_mem, w_mem_kv, g_mq, g_mk, w_fourier, g_out, w_out, g_ffn2, w_ffn2_gate, w_ffn2_up, w_ffn2_down):
    FW = w_fourier.shape[2]
    MW = w_mem_kv.shape[2] // 2
    AW = (w_in.shape[2] - FW - MW) // 3
    b16 = lambda a: a.astype(BF16)
    w = {"ffn1": (b16(w_ffn1_gate), b16(w_ffn1_up), b16(w_ffn1_down)),
         "ffn2": (b16(w_ffn2_gate), b16(w_ffn2_up), b16(w_ffn2_down)),
         "in": b16(w_in), "out": b16(w_out), "mem_kv": b16(w_mem_kv), "fourier": b16(w_fourier)}
    g = {"ffn1": g_ffn1, "ffn2": g_ffn2, "mix": g_mix, "q": g_q, "k": g_k, "mq": g_mq, "mk": g_mk,
         "mem": g_mem, "out": g_out}
    widths = (FW, AW, MW)
    return (_trunk(x_prompt, mem_prompt, w, g, widths), _trunk(x_sample, mem_sample, w, g, widths))
```

```python
import functools
import math

import jax
import jax.numpy as jnp
from jax import lax
from jax.experimental import pallas as pl
from jax.experimental.pallas import tpu as pltpu

EPS = 1e-6
HEAD_DIM = 128
ROPE_THETA = 10000.0
DILATIONS = (1, 4, 16)
HALF_WINDOW = 64
Q_BLOCK = 2 * HALF_WINDOW
K_WINDOW = 2 * Q_BLOCK
FFT_N2 = 64
FFT_ROWS = 8
NEG_BIG = -1e30
Q_SCALE = HEAD_DIM ** -0.5 * math.log2(math.e)
HEAD_BLOCKS_PER_STEP = 32
INPROJ_SLABS = 8

BF16 = jnp.bfloat16
F32 = jnp.float32

VMEM_LIMIT_BYTES = 56 * 1024 * 1024


def _params(*semantics):
    return pltpu.CompilerParams(dimension_semantics=semantics, vmem_limit_bytes=VMEM_LIMIT_BYTES)


def _rms(x, g):
    return x * lax.rsqrt(jnp.mean(x * x, axis=-1, keepdims=True) + EPS) * g


def _dot(a, b):
    return jnp.dot(a, b, preferred_element_type=F32)


def _ffn_kernel(x_ref, g_ref, wg_ref, wu_ref, wd_ref, o_ref, h_ref):
    f = pl.program_id(1)

    @pl.when(f == 0)
    def _():
        x = x_ref[...]
        h_ref[...] = _rms(x, g_ref[...]).astype(BF16)
        o_ref[...] = x

    h = h_ref[...]
    gate = _dot(h, wg_ref[...])
    up = _dot(h, wu_ref[...])
    a = (0.5 * gate * (1.0 / (1.0 + jnp.exp(-gate))) * up).astype(BF16)
    o_ref[...] += _dot(a, wd_ref[...])


def _ffn(x, g, wg, wu, wd, l, *, tm=1024, tf=512):
    T, D = x.shape
    F = wg.shape[2]
    tf = min(tf, F)
    assert T % tm == 0 and F % tf == 0
    return pl.pallas_call(
        _ffn_kernel,
        grid=(T // tm, F // tf),
        in_specs=[
            pl.BlockSpec((tm, D), lambda i, f: (i, 0)),
            pl.BlockSpec((1, D), lambda i, f: (0, 0)),
            pl.BlockSpec((None, D, tf), lambda i, f: (l, 0, f)),
            pl.BlockSpec((None, D, tf), lambda i, f: (l, 0, f)),
            pl.BlockSpec((None, tf, D), lambda i, f: (l, f, 0)),
        ],
        out_specs=pl.BlockSpec((tm, D), lambda i, f: (i, 0)),
        out_shape=jax.ShapeDtypeStruct((T, D), F32),
        scratch_shapes=[pltpu.VMEM((tm, D), BF16)],
        compiler_params=_params("parallel", "arbitrary"),
        name="ffn",
    )(x, g.reshape(1, D), wg, wu, wd)


def _inproj_kernel(x_ref, g_ref, w_ref, gq_ref, gk_ref, gmq_ref, cos_ref, sin_ref,
                   f_ref, q1_ref, k1_ref, v1_ref, q4_ref, k4_ref, v4_ref, q16_ref, k16_ref, v16_ref, mq_ref,
                   nat_ref, cm4_ref, *, tm, widths, col_chunk):
    FW, AW, MW = widths
    h = _rms(x_ref[...], g_ref[...]).astype(BF16)
    cos = cos_ref[...]
    sin = sin_ref[...]
    slabs = nat_ref.shape[0]

    def heads(col0, width, emit, fn):
        cc = min(col_chunk, width)
        for c0 in range(0, width, cc):
            p = _dot(h, w_ref[:, col0 + c0:col0 + c0 + cc])
            for j in range(cc // HEAD_DIM):
                emit(c0 // HEAD_DIM + j, fn(p[:, j * HEAD_DIM:(j + 1) * HEAD_DIM]))

    def emit_f(head, t):
        f_ref[:, head * HEAD_DIM:(head + 1) * HEAD_DIM] = t

    def class_copies(base, out1_ref, out4_ref, out16_ref):
        def emit(head, t):
            sl = slice(head * HEAD_DIM, (head + 1) * HEAD_DIM)
            slab = (base + head) % slabs
            out1_ref[:, sl] = t.astype(BF16)
            nat_ref[slab] = t
            for r4 in range(4):
                c4 = nat_ref[slab, pl.ds(r4, tm // 4, stride=4), :]
                out4_ref[r4, :, sl] = c4.astype(BF16)
                cm4_ref[slab, r4 * (tm // 4):(r4 + 1) * (tm // 4), :] = c4
            for r16 in range(16):
                start = (r16 % 4) * (tm // 4) + r16 // 4
                c16 = cm4_ref[slab, pl.ds(start, tm // 16, stride=4), :]
                out16_ref[r16, :, sl] = c16.astype(BF16)
        return emit

    def norm_rope(g_ref_, post_scale):
        def fn(t):
            t = _rms(t, g_ref_[...])
            t = t * cos + pltpu.roll(t, HEAD_DIM // 2, 1) * sin
            return t if post_scale is None else t * post_scale
        return fn

    def emit_mq(head, t):
        mq_ref[:, head * HEAD_DIM:(head + 1) * HEAD_DIM] = t.astype(BF16)

    nh = AW // HEAD_DIM
    heads(0, FW, emit_f, lambda t: t)
    heads(FW, AW, class_copies(0, q1_ref, q4_ref, q16_ref), norm_rope(gq_ref, Q_SCALE))
    heads(FW + AW, AW, class_copies(nh, k1_ref, k4_ref, k16_ref), norm_rope(gk_ref, None))
    heads(FW + 3 * AW, MW, emit_mq, lambda t: _rms(t, gmq_ref[...]))
    heads(FW + 2 * AW, AW, class_copies(2 * nh, v1_ref, v4_ref, v16_ref), lambda t: t)


def _inproj(x, g, w, l, widths, gq, gk, gmq, cos2, sin2, *, tm=512):
    B, S, D = x.shape
    FW, AW, MW = widths
    assert S % tm == 0 and tm % 256 == 0
    row = lambda w_: pl.BlockSpec((None, tm, w_), lambda b, i: (b, i, 0))
    cls = lambda d: pl.BlockSpec((None, d, tm // d, AW), lambda b, i: (b, 0, i, 0))
    vec = lambda n: pl.BlockSpec((1, n), lambda b, i: (0, 0))
    tab = pl.BlockSpec((tm, HEAD_DIM), lambda b, i: (i, 0))
    nat = jax.ShapeDtypeStruct((B, S, AW), BF16)
    c4 = jax.ShapeDtypeStruct((B, 4, S // 4, AW), BF16)
    c16 = jax.ShapeDtypeStruct((B, 16, S // 16, AW), BF16)
    return pl.pallas_call(
        functools.partial(_inproj_kernel, tm=tm, widths=widths, col_chunk=512),
        grid=(B, S // tm),
        in_specs=[row(D), vec(D),
                  pl.BlockSpec((None,) + w.shape[1:], lambda b, i: (l, 0, 0), pipeline_mode=pl.Buffered(1)),
                  vec(HEAD_DIM), vec(HEAD_DIM), vec(HEAD_DIM), tab, tab],
        out_specs=[row(FW), row(AW), row(AW), row(AW), cls(4), cls(4), cls(4), cls(16), cls(16), cls(16),
                   row(MW)],
        out_shape=[jax.ShapeDtypeStruct((B, S, FW), F32), nat, nat, nat, c4, c4, c4, c16, c16, c16,
                   jax.ShapeDtypeStruct((B, S, MW), BF16)],
        scratch_shapes=[pltpu.VMEM((INPROJ_SLABS, tm, HEAD_DIM), F32),
                        pltpu.VMEM((INPROJ_SLABS, tm, HEAD_DIM), F32)],
        compiler_params=_params("parallel", "parallel"),
        name="inproj",
    )(x, g.reshape(1, D), w, gq.reshape(1, -1), gk.reshape(1, -1), gmq.reshape(1, -1), cos2, sin2)


def _fft1_kernel(k_ref, x_ref, o_ref):
    n1, r, c = x_ref.shape
    y = _dot(k_ref[...], x_ref[...].reshape(n1 * r, c).astype(BF16))
    o_ref[...] = y.reshape(2, n1, r, c)


def _fft1(f_in, k1r):
    B, S, C = f_in.shape
    N2 = FFT_N2
    N1 = S // N2
    R = FFT_ROWS
    return pl.pallas_call(
        _fft1_kernel,
        grid=(B, N2 // R),
        in_specs=[pl.BlockSpec(k1r.shape, lambda b, j: (0, 0)),
                  pl.BlockSpec((None, N1, R, C), lambda b, j: (b, 0, j, 0))],
        out_specs=pl.BlockSpec((None, 2, N1, R, C), lambda b, j: (b, 0, 0, j, 0)),
        out_shape=jax.ShapeDtypeStruct((B, 2, N1, N2, C), F32),
        compiler_params=_params("parallel", "parallel"),
        name="fft1",
    )(k1r, f_in.reshape(B, N1, N2, C))


def _fft2_kernel(y_ref, tc_ref, ts_ref, m2_ref, p_ref, cbd_ref, sbd_ref, wf_ref, go_ref, o_ref, *, G, N2, scale):
    c = y_ref.shape[-1]
    reps = c // HEAD_DIM
    br, bi = y_ref[0], y_ref[1]
    tc = jnp.concatenate([tc_ref[...]] * reps, axis=1)
    ts = jnp.concatenate([ts_ref[...]] * reps, axis=1)
    pr = (br * tc + bi * ts).astype(BF16)
    pi = (bi * tc - br * ts).astype(BF16)
    blocks = lambda t: [t[g * N2:(g + 1) * N2] for g in range(G)]
    stacked = jnp.concatenate([jnp.concatenate(blocks(pr), axis=1), jnp.concatenate(blocks(pi), axis=1)], axis=0)
    z = _dot(m2_ref[...], stacked)
    rows = lambda t: jnp.concatenate([t[:, g * c:(g + 1) * c] for g in range(G)], axis=0)
    zr, zi = rows(z[:N2]).astype(BF16), rows(z[N2:]).astype(BF16)
    y = _dot(zr, cbd_ref[...]) + _dot(zi, sbd_ref[...])
    yf = _dot((y * scale).astype(BF16), wf_ref[...])
    o = _rms(yf, go_ref[...]).astype(BF16)
    o_ref[...] = _dot(p_ref[...], o).reshape(N2, G, c)


def _fft2(y1, tc, ts, m2, cbd, sbd, wf, l, go):
    B, _, N1, N2, C = y1.shape
    G = FFT_ROWS
    S = N1 * N2
    rows = G * N2
    full = lambda a: pl.BlockSpec(a.shape, lambda b, kb: (0,) * a.ndim)
    tw = pl.BlockSpec((rows, HEAD_DIM), lambda b, kb: (kb, 0))
    t = jnp.arange(rows, dtype=jnp.int32)
    perm = jax.nn.one_hot((t % G) * N2 + t // G, rows, dtype=BF16)
    out = pl.pallas_call(
        functools.partial(_fft2_kernel, G=G, N2=N2, scale=1.0 / math.sqrt(S * HEAD_DIM)),
        grid=(B, N1 // G),
        in_specs=[pl.BlockSpec((None, 2, rows, C), lambda b, kb: (b, 0, kb, 0)), tw, tw,
                  full(m2), full(perm), full(cbd), full(sbd),
                  pl.BlockSpec((None, C, C), lambda b, kb: (l, 0, 0)),
                  pl.BlockSpec((1, C), lambda b, kb: (0, 0))],
        out_specs=pl.BlockSpec((None, N2, G, C), lambda b, kb: (b, 0, kb, 0)),
        out_shape=jax.ShapeDtypeStruct((B, N2, N1, C), F32),
        compiler_params=_params("parallel", "parallel"),
        name="fft2",
    )(y1.reshape(B, 2, S, C), tc, ts, m2, perm, cbd, sbd, wf, go.reshape(1, C))
    return out.reshape(B, S, C)


def _dft_tables(S, C):
    N2 = FFT_N2
    N1 = S // N2
    R = FFT_ROWS

    def cs(n):
        j = jnp.arange(n, dtype=jnp.int32)
        ang = ((j[:, None] * j[None, :]) % n).astype(F32) * (2.0 * math.pi / n)
        return jnp.cos(ang), jnp.sin(ang)

    eye_r = jnp.eye(R, dtype=F32)
    c1, s1 = cs(N1)
    f1 = jnp.stack([c1, -s1])
    k1r = jnp.einsum("akn,jl->akjnl", f1, eye_r).reshape(2 * N1 * R, N1 * R).astype(BF16)
    c2, s2 = cs(N2)
    m2 = jnp.concatenate([jnp.concatenate([c2, s2], axis=1),
                          jnp.concatenate([-s2, c2], axis=1)], axis=0).astype(BF16)
    cc, sc = cs(HEAD_DIM)
    eye_g = jnp.eye(C // HEAD_DIM, dtype=F32)
    cbd = jnp.kron(eye_g, cc).astype(BF16)
    sbd = jnp.kron(eye_g, sc).astype(BF16)
    k1 = jnp.arange(N1, dtype=jnp.int32)[:, None]
    n2 = jnp.arange(N2, dtype=jnp.int32)[None, :]
    ang = (k1 * n2).astype(F32).reshape(S, 1) * (2.0 * math.pi / S)
    tc = jnp.broadcast_to(jnp.cos(ang), (S, HEAD_DIM))
    ts = jnp.broadcast_to(jnp.sin(ang), (S, HEAD_DIM))
    return k1r, m2, cbd, sbd, tc, ts


def _band_kernel(*refs, L, heads, qb, seqs, halo):
    if halo:
        q_ref, k_ref, v_ref, kp_ref, vp_ref, kn_ref, vn_ref, o_ref, st_ref = refs
    else:
        q_ref, k_ref, v_ref, o_ref, st_ref = refs
    step = pl.program_id(1)
    ones = jnp.ones((K_WINDOW, HEAD_DIM), BF16)
    lane = lax.broadcasted_iota(jnp.int32, (Q_BLOCK, HEAD_DIM), 1)
    col = lax.broadcasted_iota(jnp.int32, (Q_BLOCK, K_WINDOW), 1)
    row_minus_col = lax.broadcasted_iota(jnp.int32, (Q_BLOCK, K_WINDOW), 0) - col

    def window(main_ref, prev_ref, next_ref, b, sl):
        lo, hi = b * Q_BLOCK - HALF_WINDOW, b * Q_BLOCK + K_WINDOW - HALF_WINDOW
        parts = []
        if lo < 0:
            parts.append(prev_ref[HALF_WINDOW:, sl])
        parts.append(main_ref[max(lo, 0):min(hi, qb * Q_BLOCK), sl])
        if hi > qb * Q_BLOCK:
            parts.append(next_ref[:HALF_WINDOW, sl])
        return parts[0] if len(parts) == 1 else jnp.concatenate(parts, axis=0)

    for s in range(seqs):
        for b in range(qb):
            rows = slice(b * Q_BLOCK, (b + 1) * Q_BLOCK)
            if halo:
                key = (step * qb + b) * Q_BLOCK - HALF_WINDOW + col
                ok = (jnp.abs(row_minus_col + HALF_WINDOW) <= HALF_WINDOW) & (key >= 0) & (key < L)
                q_s, o_s, st_s = q_ref, o_ref, st_ref
            else:
                ws = min(max(b * Q_BLOCK - HALF_WINDOW, 0), L - K_WINDOW)
                ok = jnp.abs(row_minus_col + (b * Q_BLOCK - ws)) <= HALF_WINDOW
                q_s, o_s, st_s = q_ref.at[s], o_ref.at[s], st_ref.at[s]
            bias = jnp.where(ok, 0.0, NEG_BIG)
            m_mat = jnp.zeros((Q_BLOCK, HEAD_DIM), F32)
            l_mat = jnp.ones((Q_BLOCK, HEAD_DIM), F32)
            for h in range(heads):
                sl = slice(h * HEAD_DIM, (h + 1) * HEAD_DIM)
                if halo:
                    k = window(k_ref, kp_ref, kn_ref, b, sl)
                    v = window(v_ref, vp_ref, vn_ref, b, sl)
                else:
                    k = k_ref[s, ws:ws + K_WINDOW, sl]
                    v = v_ref[s, ws:ws + K_WINDOW, sl]
                sc = lax.dot_general(q_s[rows, sl], k, (((1,), (1,)), ((), ())),
                                     preferred_element_type=F32) + bias
                m = jnp.max(sc, axis=-1, keepdims=True)
                p = jnp.exp2(sc - m).astype(BF16)
                oa = _dot(p, jnp.concatenate([v, ones], axis=1))
                o_s[rows, sl] = oa[:, :HEAD_DIM].astype(BF16)
                m_mat = jnp.where(lane == h, m, m_mat)
                l_mat = jnp.where(lane == h, oa[:, HEAD_DIM:], l_mat)
            st_s[rows, :HEAD_DIM] = m_mat
            st_s[rows, HEAD_DIM:] = l_mat


def _band_attention(q, k, v):
    G, L, AW = q.shape
    heads = AW // HEAD_DIM
    nblk = L // Q_BLOCK
    assert L % Q_BLOCK == 0 and L >= K_WINDOW and heads <= HEAD_DIM
    qb = max(1, min(HEAD_BLOCKS_PER_STEP // heads, nblk))
    assert nblk % qb == 0
    out_shape = [jax.ShapeDtypeStruct((G, L, AW), BF16), jax.ShapeDtypeStruct((G, L, 2 * HEAD_DIM), F32)]
    if qb == nblk:
        seqs = max(1, min(HEAD_BLOCKS_PER_STEP // (heads * qb), G))
        assert G % seqs == 0
        blk = lambda w: pl.BlockSpec((seqs, L, w), lambda g, n: (g, 0, 0))
        return pl.pallas_call(
            functools.partial(_band_kernel, L=L, heads=heads, qb=qb, seqs=seqs, halo=False),
            grid=(G // seqs, 1),
            in_specs=[blk(AW)] * 3,
            out_specs=[blk(AW), blk(2 * HEAD_DIM)],
            out_shape=out_shape,
            compiler_params=_params("parallel", "parallel"),
            name="band_attention",
        )(q, k, v)
    main = pl.BlockSpec((None, qb * Q_BLOCK, AW), lambda g, n: (g, n, 0))
    prev = pl.BlockSpec((None, Q_BLOCK, AW), lambda g, n: (g, jnp.maximum(n * qb - 1, 0), 0))
    nxt = pl.BlockSpec((None, Q_BLOCK, AW), lambda g, n: (g, jnp.minimum((n + 1) * qb, nblk - 1), 0))
    return pl.pallas_call(
        functools.partial(_band_kernel, L=L, heads=heads, qb=qb, seqs=1, halo=True),
        grid=(G, nblk // qb),
        in_specs=[main, main, main, prev, prev, nxt, nxt],
        out_specs=[main, pl.BlockSpec((None, qb * Q_BLOCK, 2 * HEAD_DIM), lambda g, n: (g, n, 0))],
        out_shape=out_shape,
        compiler_params=_params("parallel", "parallel"),
        name="band_attention",
    )(q, k, v, k, v, k, v)


def _memkv_kernel(m_ref, g_ref, w_ref, gk_ref, k_ref, v_ref, *, MW):
    h = _rms(m_ref[...], g_ref[...]).astype(BF16)
    kv = _dot(h, w_ref[...])
    for j in range(MW // HEAD_DIM):
        sl = slice(j * HEAD_DIM, (j + 1) * HEAD_DIM)
        k_ref[:, sl] = _rms(kv[:, sl], gk_ref[...]).astype(BF16)
    v_ref[...] = kv[:, MW:].astype(BF16)


def _memkv(mem, g, w, l, gk, *, tm=256):
    B, M, D = mem.shape
    MW = w.shape[2] // 2
    rows = B * M
    tm = min(tm, rows)
    assert rows % tm == 0
    out = pl.BlockSpec((tm, MW), lambda i: (i, 0))
    return pl.pallas_call(
        functools.partial(_memkv_kernel, MW=MW),
        grid=(rows // tm,),
        in_specs=[pl.BlockSpec((tm, D), lambda i: (i, 0)),
                  pl.BlockSpec((1, D), lambda i: (0, 0)),
                  pl.BlockSpec((None,) + w.shape[1:], lambda i: (l, 0, 0)),
                  pl.BlockSpec((1, HEAD_DIM), lambda i: (0, 0))],
        out_specs=[out, out],
        out_shape=[jax.ShapeDtypeStruct((rows, MW), BF16)] * 2,
        compiler_params=_params("parallel"),
        name="memkv",
    )(mem.reshape(rows, D), g.reshape(1, D), w, gk.reshape(1, -1))


def _memattn_kernel(q_ref, k_ref, v_ref, go_ref, o_ref, y_ref, *, MW):
    scale = HEAD_DIM ** -0.5
    for j in range(MW // HEAD_DIM):
        sl = slice(j * HEAD_DIM, (j + 1) * HEAD_DIM)
        s = lax.dot_general(q_ref[:, sl], k_ref[:, sl], (((1,), (1,)), ((), ())),
                            preferred_element_type=F32) * scale
        m = jnp.max(s, axis=-1, keepdims=True)
        p = jnp.exp(s - m)
        l = jnp.sum(p, axis=-1, keepdims=True)
        y_ref[:, sl] = _dot(p.astype(BF16), v_ref[:, sl]) * (1.0 / l)
    o_ref[...] = _rms(y_ref[...], go_ref[...]).astype(BF16)


def _memattn(mq, mk, mv, go, *, tm=512):
    B, S, MW = mq.shape
    M = mk.shape[1]
    assert S % tm == 0
    row = pl.BlockSpec((None, tm, MW), lambda b, i: (b, i, 0))
    mem = pl.BlockSpec((None, M, MW), lambda b, i: (b, 0, 0))
    return pl.pallas_call(
        functools.partial(_memattn_kernel, MW=MW),
        grid=(B, S // tm),
        in_specs=[row, mem, mem, pl.BlockSpec((1, MW), lambda b, i: (0, 0))],
        out_specs=row,
        out_shape=jax.ShapeDtypeStruct((B, S, MW), BF16),
        scratch_shapes=[pltpu.VMEM((tm, MW), F32)],
        compiler_params=_params("parallel", "parallel"),
        name="memattn",
    )(mq, mk, mv, go.reshape(1, MW))


def _outproj_kernel(x_ref, yf_ref, o1_ref, o4_ref, o16_ref, l1_ref, l4_ref, l16_ref, ym_ref, go_ref,
                    p4_ref, p16_ref, e_ref, w_ref, out_ref, m4_ref, d4_ref, m16_ref, d16_ref, *, tm, AW):
    FW = yf_ref.shape[-1]
    for dil, src, m_dst, d_dst in ((4, l4_ref, m4_ref, d4_ref), (16, l16_ref, m16_ref, d16_ref)):
        for r in range(dil):
            m_dst[pl.ds(r, tm // dil, stride=dil), :] = src[r, :, :HEAD_DIM]
            d_dst[pl.ds(r, tm // dil, stride=dil), :] = src[r, :, HEAD_DIM:]
    o1 = o1_ref[...]
    o4 = _dot(p4_ref[...], o4_ref[...].reshape(tm, AW))
    o16 = _dot(p16_ref[...], o16_ref[...].reshape(tm, AW))

    m1, m4, m16 = l1_ref[:, :HEAD_DIM], m4_ref[...], m16_ref[...]
    top = jnp.maximum(jnp.maximum(m1, m4), m16)
    e1, e4, e16 = jnp.exp2(m1 - top), jnp.exp2(m4 - top), jnp.exp2(m16 - top)
    inv = 1.0 / (e1 * l1_ref[:, HEAD_DIM:] + e4 * d4_ref[...] + e16 * d16_ref[...])

    def spread(w):
        hi = w.astype(BF16)
        lo = (w - hi.astype(F32)).astype(BF16)
        return _dot(jnp.concatenate([hi, lo], axis=1), e_ref[...])

    y = spread(e1 * inv) * o1 + spread(e4 * inv) * o4 + spread(e16 * inv) * o16
    ya = _rms(y, go_ref[...]).astype(BF16)
    out_ref[...] = (x_ref[...] + _dot(yf_ref[...].astype(BF16), w_ref[:FW]) + _dot(ya, w_ref[FW:FW + AW])
                    + _dot(ym_ref[...], w_ref[FW + AW:]))


def _unpermute_matrix(tm, dil):
    t = jnp.arange(tm, dtype=jnp.int32)
    return jax.nn.one_hot((t % dil) * (tm // dil) + t // dil, tm, dtype=BF16)


def _outproj(x, yf, o, lse, ym, go_a, w, l, *, tm=256):
    B, S, D = x.shape
    FW, AW, MW = yf.shape[-1], o[0].shape[-1], ym.shape[-1]
    assert S % tm == 0 and tm % 256 == 0 and all(t.shape[-1] == 2 * HEAD_DIM for t in lse)
    row = lambda w_: pl.BlockSpec((None, tm, w_), lambda b, i: (b, i, 0))
    cls = lambda d, w_: pl.BlockSpec((None, d, tm // d, w_), lambda b, i: (b, 0, i, 0))
    full = lambda a: pl.BlockSpec(a.shape, lambda b, i: (0,) * a.ndim)
    p4, p16 = _unpermute_matrix(tm, 4), _unpermute_matrix(tm, 16)
    spread = (jnp.arange(2 * HEAD_DIM, dtype=jnp.int32)[:, None] % HEAD_DIM
              == jnp.arange(AW, dtype=jnp.int32)[None, :] // HEAD_DIM).astype(BF16)
    return pl.pallas_call(
        functools.partial(_outproj_kernel, tm=tm, AW=AW),
        grid=(B, S // tm),
        in_specs=[row(D), row(FW), row(AW), cls(4, AW), cls(16, AW), row(2 * HEAD_DIM), cls(4, 2 * HEAD_DIM),
                  cls(16, 2 * HEAD_DIM), row(MW), pl.BlockSpec((1, AW), lambda b, i: (0, 0)),
                  full(p4), full(p16), full(spread),
                  pl.BlockSpec((None,) + w.shape[1:], lambda b, i: (l, 0, 0))],
        out_specs=row(D),
        out_shape=jax.ShapeDtypeStruct((B, S, D), F32),
        scratch_shapes=[pltpu.VMEM((tm, HEAD_DIM), F32)] * 4,
        compiler_params=_params("parallel", "parallel"),
        name="outproj",
    )(x, yf, *o, *lse, ym, go_a.reshape(1, AW), p4, p16, spread, w)


def _rope_tables(S):
    inv = ROPE_THETA ** (-jnp.arange(0, HEAD_DIM, 2, dtype=F32) / HEAD_DIM)
    ang = jnp.arange(S, dtype=F32)[:, None] * inv[None, :]
    cos, sin = jnp.cos(ang), jnp.sin(ang)
    return jnp.concatenate([cos, cos], axis=1), jnp.concatenate([-sin, sin], axis=1)


def _trunk(x, mem, w, g, widths):
    B, S, D = x.shape
    FW, AW, MW = widths
    M = mem.shape[1]
    cos2, sin2 = _rope_tables(S)
    k1r, m2, cbd, sbd, tc, ts = _dft_tables(S, FW)
    for l in range(w["in"].shape[0]):
        x = _ffn(x.reshape(B * S, D), g["ffn1"][l], *w["ffn1"], l).reshape(B, S, D)

        f_in, *qkv, mq = _inproj(x, g["mix"][l], w["in"], l, widths, g["q"][l], g["k"][l], g["mq"][l],
                                 cos2, sin2)

        g_out = g["out"][l]
        y_f = _fft2(_fft1(f_in, k1r), tc, ts, m2, cbd, sbd, w["fourier"], l, g_out[:FW])

        outs, lses = [], []
        for i, dil in enumerate(DILATIONS):
            q, k, v = (t.reshape(B * dil, S // dil, AW) for t in qkv[3 * i:3 * i + 3])
            o, lse = _band_attention(q, k, v)
            lead = (B, S) if dil == 1 else (B, dil, S // dil)
            outs.append(o.reshape(*lead, AW))
            lses.append(lse.reshape(*lead, 2 * HEAD_DIM))

        mk, mv = _memkv(mem, g["mem"][l], w["mem_kv"], l, g["mk"][l])
        y_m = _memattn(mq, mk.reshape(B, M, MW), mv.reshape(B, M, MW), g_out[FW + AW:])

        x = _outproj(x, y_f, outs, lses, y_m, g_out[FW:FW + AW], w["out"], l)
        x = _ffn(x.reshape(B * S, D), g["ffn2"][l], *w["ffn2"], l).reshape(B, S, D)
    return x


def kernel(x_prompt, x_sample, mem_prompt, mem_sample, g_ffn1, w_ffn1_gate, w_ffn1_up, w_ffn1_down, g_mix, w_in, g_q, g_k, g_mem, w_mem_kv, g_mq, g_mk, w_fourier, g_out, w_out, g_ffn2, w_ffn2_gate, w_ffn2_up, w_ffn2_down):
    FW = w_fourier.shape[2]
    MW = w_mem_kv.shape[2] // 2
    AW = (w_in.shape[2] - FW - MW) // 3
    b16 = lambda a: a.astype(BF16)
    w = {"ffn1": (b16(w_ffn1_gate), b16(w_ffn1_up), b16(w_ffn1_down)),
         "ffn2": (b16(w_ffn2_gate), b16(w_ffn2_up), b16(w_ffn2_down)),
         "in": b16(w_in), "out": b16(w_out), "mem_kv": b16(w_mem_kv), "fourier": b16(w_fourier)}
    g = {"ffn1": g_ffn1, "ffn2": g_ffn2, "mix": g_mix, "q": g_q, "k": g_k, "mq": g_mq, "mk": g_mk,
         "mem": g_mem, "out": g_out}
    widths = (FW, AW, MW)
    return (_trunk(x_prompt, mem_prompt, w, g, widths), _trunk(x_sample, mem_sample, w, g, widths))
```

```python
import functools
import math

import jax
import jax.numpy as jnp
from jax import lax
from jax.experimental import pallas as pl
from jax.experimental.pallas import tpu as pltpu

EPS = 1e-6
HEAD_DIM = 128
ROPE_THETA = 10000.0
DILATIONS = (1, 4, 16)
HALF_WINDOW = 64
Q_BLOCK = 2 * HALF_WINDOW
K_WINDOW = 2 * Q_BLOCK
FFT_N2 = 64
FFT_ROWS = 8
NEG_BIG = -1e30
Q_SCALE = HEAD_DIM ** -0.5 * math.log2(math.e)
HEAD_BLOCKS_PER_STEP = 32
STAT_SPLIT = HEAD_DIM // 2
INPROJ_SLABS = 8

BF16 = jnp.bfloat16
F32 = jnp.float32

VMEM_LIMIT_BYTES = 56 * 1024 * 1024


def _params(*semantics):
    return pltpu.CompilerParams(dimension_semantics=semantics, vmem_limit_bytes=VMEM_LIMIT_BYTES)


def _rms(x, g):
    return x * lax.rsqrt(jnp.mean(x * x, axis=-1, keepdims=True) + EPS) * g


def _dot(a, b):
    return jnp.dot(a, b, preferred_element_type=F32)


def _ffn_kernel(x_ref, g_ref, wg_ref, wu_ref, wd_ref, o_ref, h_ref):
    f = pl.program_id(1)

    @pl.when(f == 0)
    def _():
        x = x_ref[...]
        h_ref[...] = _rms(x, g_ref[...]).astype(BF16)
        o_ref[...] = x

    h = h_ref[...]
    gate = _dot(h, wg_ref[...])
    up = _dot(h, wu_ref[...])
    a = (0.5 * gate * (1.0 / (1.0 + jnp.exp(-gate))) * up).astype(BF16)
    o_ref[...] += _dot(a, wd_ref[...])


def _ffn(x, g, wg, wu, wd, l, *, tm=1024, tf=512):
    T, D = x.shape
    F = wg.shape[2]
    tf = min(tf, F)
    assert T % tm == 0 and F % tf == 0
    return pl.pallas_call(
        _ffn_kernel,
        grid=(T // tm, F // tf),
        in_specs=[
            pl.BlockSpec((tm, D), lambda i, f: (i, 0)),
            pl.BlockSpec((1, D), lambda i, f: (0, 0)),
            pl.BlockSpec((None, D, tf), lambda i, f: (l, 0, f)),
            pl.BlockSpec((None, D, tf), lambda i, f: (l, 0, f)),
            pl.BlockSpec((None, tf, D), lambda i, f: (l, f, 0)),
        ],
        out_specs=pl.BlockSpec((tm, D), lambda i, f: (i, 0)),
        out_shape=jax.ShapeDtypeStruct((T, D), F32),
        scratch_shapes=[pltpu.VMEM((tm, D), BF16)],
        compiler_params=_params("parallel", "arbitrary"),
        name="ffn",
    )(x, g.reshape(1, D), wg, wu, wd)


def _inproj_kernel(x_ref, g_ref, w_ref, gq_ref, gk_ref, gmq_ref, cos_ref, sin_ref,
                   f_ref, q1_ref, k1_ref, v1_ref, q4_ref, k4_ref, v4_ref, q16_ref, k16_ref, v16_ref, mq_ref,
                   nat_ref, cm4_ref, *, tm, widths, col_chunk):
    FW, AW, MW = widths
    h = _rms(x_ref[...], g_ref[...]).astype(BF16)
    cos = cos_ref[...]
    sin = sin_ref[...]
    slabs = nat_ref.shape[0]

    def heads(col0, width, emit, fn):
        cc = min(col_chunk, width)
        for c0 in range(0, width, cc):
            p = _dot(h, w_ref[:, col0 + c0:col0 + c0 + cc])
            for j in range(cc // HEAD_DIM):
                emit(c0 // HEAD_DIM + j, fn(p[:, j * HEAD_DIM:(j + 1) * HEAD_DIM]))

    def emit_f(head, t):
        f_ref[:, head * HEAD_DIM:(head + 1) * HEAD_DIM] = t

    def class_copies(base, out1_ref, out4_ref, out16_ref):
        def emit(head, t):
            sl = slice(head * HEAD_DIM, (head + 1) * HEAD_DIM)
            slab = (base + head) % slabs
            out1_ref[:, sl] = t.astype(BF16)
            nat_ref[slab] = t
            for r4 in range(4):
                c4 = nat_ref[slab, pl.ds(r4, tm // 4, stride=4), :]
                out4_ref[r4, :, sl] = c4.astype(BF16)
                cm4_ref[slab, r4 * (tm // 4):(r4 + 1) * (tm // 4), :] = c4
            for r16 in range(16):
                start = (r16 % 4) * (tm // 4) + r16 // 4
                c16 = cm4_ref[slab, pl.ds(start, tm // 16, stride=4), :]
                out16_ref[r16, :, sl] = c16.astype(BF16)
        return emit

    def norm_rope(g_ref_, post_scale):
        def fn(t):
            t = _rms(t, g_ref_[...])
            t = t * cos + pltpu.roll(t, HEAD_DIM // 2, 1) * sin
            return t if post_scale is None else t * post_scale
        return fn

    def emit_mq(head, t):
        mq_ref[:, head * HEAD_DIM:(head + 1) * HEAD_DIM] = t.astype(BF16)

    nh = AW // HEAD_DIM
    heads(0, FW, emit_f, lambda t: t)
    heads(FW, AW, class_copies(0, q1_ref, q4_ref, q16_ref), norm_rope(gq_ref, Q_SCALE))
    heads(FW + AW, AW, class_copies(nh, k1_ref, k4_ref, k16_ref), norm_rope(gk_ref, None))
    heads(FW + 3 * AW, MW, emit_mq, lambda t: _rms(t, gmq_ref[...]))
    heads(FW + 2 * AW, AW, class_copies(2 * nh, v1_ref, v4_ref, v16_ref), lambda t: t)


def _inproj(x, g, w, l, widths, gq, gk, gmq, cos2, sin2, *, tm=512):
    B, S, D = x.shape
    FW, AW, MW = widths
    assert S % tm == 0 and tm % 256 == 0
    row = lambda w_: pl.BlockSpec((None, tm, w_), lambda b, i: (b, i, 0))
    cls = lambda d: pl.BlockSpec((None, d, tm // d, AW), lambda b, i: (b, 0, i, 0))
    vec = lambda n: pl.BlockSpec((1, n), lambda b, i: (0, 0))
    tab = pl.BlockSpec((tm, HEAD_DIM), lambda b, i: (i, 0))
    nat = jax.ShapeDtypeStruct((B, S, AW), BF16)
    c4 = jax.ShapeDtypeStruct((B, 4, S // 4, AW), BF16)
    c16 = jax.ShapeDtypeStruct((B, 16, S // 16, AW), BF16)
    return pl.pallas_call(
        functools.partial(_inproj_kernel, tm=tm, widths=widths, col_chunk=512),
        grid=(B, S // tm),
        in_specs=[row(D), vec(D),
                  pl.BlockSpec((None,) + w.shape[1:], lambda b, i: (l, 0, 0), pipeline_mode=pl.Buffered(1)),
                  vec(HEAD_DIM), vec(HEAD_DIM), vec(HEAD_DIM), tab, tab],
        out_specs=[row(FW), row(AW), row(AW), row(AW), cls(4), cls(4), cls(4), cls(16), cls(16), cls(16),
                   row(MW)],
        out_shape=[jax.ShapeDtypeStruct((B, S, FW), F32), nat, nat, nat, c4, c4, c4, c16, c16, c16,
                   jax.ShapeDtypeStruct((B, S, MW), BF16)],
        scratch_shapes=[pltpu.VMEM((INPROJ_SLABS, tm, HEAD_DIM), F32),
                        pltpu.VMEM((INPROJ_SLABS, tm, HEAD_DIM), F32)],
        compiler_params=_params("parallel", "parallel"),
        name="inproj",
    )(x, g.reshape(1, D), w, gq.reshape(1, -1), gk.reshape(1, -1), gmq.reshape(1, -1), cos2, sin2)


def _fft1_kernel(k_ref, x_ref, o_ref):
    n1, r, c = x_ref.shape
    y = _dot(k_ref[...], x_ref[...].reshape(n1 * r, c).astype(BF16))
    o_ref[...] = y.reshape(2, n1, r, c)


def _fft1(f_in, k1r):
    B, S, C = f_in.shape
    N2 = FFT_N2
    N1 = S // N2
    R = FFT_ROWS
    return pl.pallas_call(
        _fft1_kernel,
        grid=(B, N2 // R),
        in_specs=[pl.BlockSpec(k1r.shape, lambda b, j: (0, 0)),
                  pl.BlockSpec((None, N1, R, C), lambda b, j: (b, 0, j, 0))],
        out_specs=pl.BlockSpec((None, 2, N1, R, C), lambda b, j: (b, 0, 0, j, 0)),
        out_shape=jax.ShapeDtypeStruct((B, 2, N1, N2, C), F32),
        compiler_params=_params("parallel", "parallel"),
        name="fft1",
    )(k1r, f_in.reshape(B, N1, N2, C))


def _fft2_kernel(y_ref, tc_ref, ts_ref, m2_ref, p_ref, cbd_ref, sbd_ref, wf_ref, go_ref, o_ref, *, G, N2, scale):
    c = y_ref.shape[-1]
    reps = c // HEAD_DIM
    br, bi = y_ref[0], y_ref[1]
    tc = jnp.concatenate([tc_ref[...]] * reps, axis=1)
    ts = jnp.concatenate([ts_ref[...]] * reps, axis=1)
    pr = (br * tc + bi * ts).astype(BF16)
    pi = (bi * tc - br * ts).astype(BF16)
    blocks = lambda t: [t[g * N2:(g + 1) * N2] for g in range(G)]
    stacked = jnp.concatenate([jnp.concatenate(blocks(pr), axis=1), jnp.concatenate(blocks(pi), axis=1)], axis=0)
    z = _dot(m2_ref[...], stacked)
    rows = lambda t: jnp.concatenate([t[:, g * c:(g + 1) * c] for g in range(G)], axis=0)
    zr, zi = rows(z[:N2]).astype(BF16), rows(z[N2:]).astype(BF16)
    y = _dot(zr, cbd_ref[...]) + _dot(zi, sbd_ref[...])
    yf = _dot((y * scale).astype(BF16), wf_ref[...])
    o = _rms(yf, go_ref[...]).astype(BF16)
    o_ref[...] = _dot(p_ref[...], o).reshape(N2, G, c)


def _fft2(y1, tc, ts, m2, cbd, sbd, wf, l, go):
    B, _, N1, N2, C = y1.shape
    G = FFT_ROWS
    S = N1 * N2
    rows = G * N2
    full = lambda a: pl.BlockSpec(a.shape, lambda b, kb: (0,) * a.ndim)
    tw = pl.BlockSpec((rows, HEAD_DIM), lambda b, kb: (kb, 0))
    t = jnp.arange(rows, dtype=jnp.int32)
    perm = jax.nn.one_hot((t % G) * N2 + t // G, rows, dtype=BF16)
    out = pl.pallas_call(
        functools.partial(_fft2_kernel, G=G, N2=N2, scale=1.0 / math.sqrt(S * HEAD_DIM)),
        grid=(B, N1 // G),
        in_specs=[pl.BlockSpec((None, 2, rows, C), lambda b, kb: (b, 0, kb, 0)), tw, tw,
                  full(m2), full(perm), full(cbd), full(sbd),
                  pl.BlockSpec((None, C, C), lambda b, kb: (l, 0, 0)),
                  pl.BlockSpec((1, C), lambda b, kb: (0, 0))],
        out_specs=pl.BlockSpec((None, N2, G, C), lambda b, kb: (b, 0, kb, 0)),
        out_shape=jax.ShapeDtypeStruct((B, N2, N1, C), F32),
        compiler_params=_params("parallel", "parallel"),
        name="fft2",
    )(y1.reshape(B, 2, S, C), tc, ts, m2, perm, cbd, sbd, wf, go.reshape(1, C))
    return out.reshape(B, S, C)


def _dft_tables(S, C):
    N2 = FFT_N2
    N1 = S // N2
    R = FFT_ROWS

    def cs(n):
        j = jnp.arange(n, dtype=jnp.int32)
        ang = ((j[:, None] * j[None, :]) % n).astype(F32) * (2.0 * math.pi / n)
        return jnp.cos(ang), jnp.sin(ang)

    c1, s1 = cs(N1)
    f1 = jnp.concatenate([c1, -s1], axis=0)
    ri = jnp.arange(2 * N1 * R, dtype=jnp.int32)[:, None]
    ci = jnp.arange(N1 * R, dtype=jnp.int32)[None, :]
    pick_row = (ri // R == jnp.arange(2 * N1, dtype=jnp.int32)[None, :]).astype(F32)
    pick_col = (jnp.arange(N1, dtype=jnp.int32)[:, None] == ci // R).astype(F32)
    spread = jnp.dot(jnp.dot(pick_row, f1, precision=lax.Precision.HIGHEST), pick_col,
                     precision=lax.Precision.HIGHEST)
    k1r = jnp.where(ri % R == ci % R, spread, 0.0).astype(BF16)
    c2, s2 = cs(N2)
    m2 = jnp.concatenate([jnp.concatenate([c2, s2], axis=1),
                          jnp.concatenate([-s2, c2], axis=1)], axis=0).astype(BF16)
    cc, sc = cs(HEAD_DIM)
    eye_g = jnp.eye(C // HEAD_DIM, dtype=F32)
    cbd = jnp.kron(eye_g, cc).astype(BF16)
    sbd = jnp.kron(eye_g, sc).astype(BF16)
    k1 = jnp.arange(N1, dtype=jnp.int32)[:, None]
    n2 = jnp.arange(N2, dtype=jnp.int32)[None, :]
    ang = (k1 * n2).astype(F32).reshape(S, 1) * (2.0 * math.pi / S)
    tc = jnp.broadcast_to(jnp.cos(ang), (S, HEAD_DIM))
    ts = jnp.broadcast_to(jnp.sin(ang), (S, HEAD_DIM))
    return k1r, m2, cbd, sbd, tc, ts


def _band_kernel(*refs, L, heads, qb, seqs, halo):
    if halo:
        q_ref, k_ref, v_ref, kp_ref, vp_ref, kn_ref, vn_ref, o_ref, st_ref = refs
    else:
        q_ref, k_ref, v_ref, o_ref, st_ref = refs
    step = pl.program_id(1)
    ones = jnp.ones((K_WINDOW, HEAD_DIM), BF16)
    lane = lax.broadcasted_iota(jnp.int32, (Q_BLOCK, HEAD_DIM), 1)
    col = lax.broadcasted_iota(jnp.int32, (Q_BLOCK, K_WINDOW), 1)
    row_minus_col = lax.broadcasted_iota(jnp.int32, (Q_BLOCK, K_WINDOW), 0) - col

    def window(main_ref, prev_ref, next_ref, b, sl):
        lo, hi = b * Q_BLOCK - HALF_WINDOW, b * Q_BLOCK + K_WINDOW - HALF_WINDOW
        parts = []
        if lo < 0:
            parts.append(prev_ref[:, sl])
        parts.append(main_ref[max(lo, 0):min(hi, qb * Q_BLOCK), sl])
        if hi > qb * Q_BLOCK:
            parts.append(next_ref[:, sl])
        return parts[0] if len(parts) == 1 else jnp.concatenate(parts, axis=0)

    for s in range(seqs):
        for b in range(qb):
            rows = slice(b * Q_BLOCK, (b + 1) * Q_BLOCK)
            if halo:
                key = (step * qb + b) * Q_BLOCK - HALF_WINDOW + col
                ok = (jnp.abs(row_minus_col + HALF_WINDOW) <= HALF_WINDOW) & (key >= 0) & (key < L)
                q_s, o_s, st_s = q_ref, o_ref, st_ref
            else:
                ws = min(max(b * Q_BLOCK - HALF_WINDOW, 0), L - K_WINDOW)
                ok = jnp.abs(row_minus_col + (b * Q_BLOCK - ws)) <= HALF_WINDOW
                q_s, o_s, st_s = q_ref.at[s], o_ref.at[s], st_ref.at[s]
            bias = jnp.where(ok, 0.0, NEG_BIG)
            stats = jnp.where(lane < STAT_SPLIT, 0.0, 1.0)
            for h in range(heads):
                sl = slice(h * HEAD_DIM, (h + 1) * HEAD_DIM)
                if halo:
                    k = window(k_ref, kp_ref, kn_ref, b, sl)
                    v = window(v_ref, vp_ref, vn_ref, b, sl)
                else:
                    k = k_ref[s, ws:ws + K_WINDOW, sl]
                    v = v_ref[s, ws:ws + K_WINDOW, sl]
                sc = lax.dot_general(q_s[rows, sl], k, (((1,), (1,)), ((), ())),
                                     preferred_element_type=F32) + bias
                m = jnp.max(sc, axis=-1, keepdims=True)
                p = jnp.exp2(sc - m).astype(BF16)
                oa = _dot(p, jnp.concatenate([v, ones], axis=1))
                o_s[rows, sl] = oa[:, :HEAD_DIM].astype(BF16)
                stats = jnp.where(lane == h, m, jnp.where(lane == STAT_SPLIT + h, oa[:, HEAD_DIM:], stats))
            st_s[rows, :] = stats


def _band_attention(q, k, v):
    G, L, AW = q.shape
    heads = AW // HEAD_DIM
    nblk = L // Q_BLOCK
    assert L % Q_BLOCK == 0 and L >= K_WINDOW and heads <= STAT_SPLIT
    qb = max(1, min(HEAD_BLOCKS_PER_STEP // heads, nblk))
    assert nblk % qb == 0
    out_shape = [jax.ShapeDtypeStruct((G, L, AW), BF16), jax.ShapeDtypeStruct((G, L, HEAD_DIM), F32)]
    if qb == nblk:
        seqs = max(1, min(HEAD_BLOCKS_PER_STEP // (heads * qb), G))
        assert G % seqs == 0
        blk = lambda w: pl.BlockSpec((seqs, L, w), lambda g, n: (g, 0, 0))
        return pl.pallas_call(
            functools.partial(_band_kernel, L=L, heads=heads, qb=qb, seqs=seqs, halo=False),
            grid=(G // seqs, 1),
            in_specs=[blk(AW)] * 3,
            out_specs=[blk(AW), blk(HEAD_DIM)],
            out_shape=out_shape,
            compiler_params=_params("parallel", "parallel"),
            name="band_attention",
        )(q, k, v)
    main = pl.BlockSpec((None, qb * Q_BLOCK, AW), lambda g, n: (g, n, 0))
    halo_per_step = qb * Q_BLOCK // HALF_WINDOW
    prev = pl.BlockSpec((None, HALF_WINDOW, AW), lambda g, n: (g, jnp.maximum(n * halo_per_step - 1, 0), 0))
    nxt = pl.BlockSpec((None, HALF_WINDOW, AW),
                       lambda g, n: (g, jnp.minimum((n + 1) * halo_per_step, L // HALF_WINDOW - 1), 0))
    return pl.pallas_call(
        functools.partial(_band_kernel, L=L, heads=heads, qb=qb, seqs=1, halo=True),
        grid=(G, nblk // qb),
        in_specs=[main, main, main, prev, prev, nxt, nxt],
        out_specs=[main, pl.BlockSpec((None, qb * Q_BLOCK, HEAD_DIM), lambda g, n: (g, n, 0))],
        out_shape=out_shape,
        compiler_params=_params("parallel", "parallel"),
        name="band_attention",
    )(q, k, v, k, v, k, v)


def _memkv_kernel(m_ref, g_ref, w_ref, gk_ref, k_ref, v_ref, *, MW):
    h = _rms(m_ref[...], g_ref[...]).astype(BF16)
    kv = _dot(h, w_ref[...])
    for j in range(MW // HEAD_DIM):
        sl = slice(j * HEAD_DIM, (j + 1) * HEAD_DIM)
        k_ref[:, sl] = _rms(kv[:, sl], gk_ref[...]).astype(BF16)
    v_ref[...] = kv[:, MW:].astype(BF16)


def _memkv(mem, g, w, l, gk, *, tm=256):
    B, M, D = mem.shape
    MW = w.shape[2] // 2
    rows = B * M
    tm = min(tm, rows)
    assert rows % tm == 0
    out = pl.BlockSpec((tm, MW), lambda i: (i, 0))
    return pl.pallas_call(
        functools.partial(_memkv_kernel, MW=MW),
        grid=(rows // tm,),
        in_specs=[pl.BlockSpec((tm, D), lambda i: (i, 0)),
                  pl.BlockSpec((1, D), lambda i: (0, 0)),
                  pl.BlockSpec((None,) + w.shape[1:], lambda i: (l, 0, 0)),
                  pl.BlockSpec((1, HEAD_DIM), lambda i: (0, 0))],
        out_specs=[out, out],
        out_shape=[jax.ShapeDtypeStruct((rows, MW), BF16)] * 2,
        compiler_params=_params("parallel"),
        name="memkv",
    )(mem.reshape(rows, D), g.reshape(1, D), w, gk.reshape(1, -1))


def _memattn_kernel(q_ref, k_ref, v_ref, go_ref, o_ref, y_ref, *, MW):
    scale = HEAD_DIM ** -0.5
    for j in range(MW // HEAD_DIM):
        sl = slice(j * HEAD_DIM, (j + 1) * HEAD_DIM)
        s = lax.dot_general(q_ref[:, sl], k_ref[:, sl], (((1,), (1,)), ((), ())),
                            preferred_element_type=F32) * scale
        m = jnp.max(s, axis=-1, keepdims=True)
        p = jnp.exp(s - m)
        l = jnp.sum(p, axis=-1, keepdims=True)
        y_ref[:, sl] = _dot(p.astype(BF16), v_ref[:, sl]) * (1.0 / l)
    o_ref[...] = _rms(y_ref[...], go_ref[...]).astype(BF16)


def _memattn(mq, mk, mv, go, *, tm=512):
    B, S, MW = mq.shape
    M = mk.shape[1]
    assert S % tm == 0
    row = pl.BlockSpec((None, tm, MW), lambda b, i: (b, i, 0))
    mem = pl.BlockSpec((None, M, MW), lambda b, i: (b, 0, 0))
    return pl.pallas_call(
        functools.partial(_memattn_kernel, MW=MW),
        grid=(B, S // tm),
        in_specs=[row, mem, mem, pl.BlockSpec((1, MW), lambda b, i: (0, 0))],
        out_specs=row,
        out_shape=jax.ShapeDtypeStruct((B, S, MW), BF16),
        scratch_shapes=[pltpu.VMEM((tm, MW), F32)],
        compiler_params=_params("parallel", "parallel"),
        name="memattn",
    )(mq, mk, mv, go.reshape(1, MW))


def _outproj_kernel(x_ref, yf_ref, o1_ref, o4_ref, o16_ref, l1_ref, l4_ref, l16_ref, ym_ref, go_ref,
                    p4_ref, p16_ref, e_ref, w_ref, out_ref, s4_ref, s16_ref, *, tm, AW):
    FW = yf_ref.shape[-1]
    for dil, src, dst in ((4, l4_ref, s4_ref), (16, l16_ref, s16_ref)):
        for r in range(dil):
            dst[pl.ds(r, tm // dil, stride=dil), :] = src[r]
    o1 = o1_ref[...]
    o4 = _dot(p4_ref[...], o4_ref[...].reshape(tm, AW))
    o16 = _dot(p16_ref[...], o16_ref[...].reshape(tm, AW))

    low = lax.broadcasted_iota(jnp.int32, (tm, HEAD_DIM), 1) < STAT_SPLIT

    def split(st):
        return jnp.where(low, st, 0.0), jnp.where(low, pltpu.roll(st, STAT_SPLIT, 1), 1.0)

    (m1, d1), (m4, d4), (m16, d16) = split(l1_ref[...]), split(s4_ref[...]), split(s16_ref[...])
    top = jnp.maximum(jnp.maximum(m1, m4), m16)
    e1, e4, e16 = jnp.exp2(m1 - top), jnp.exp2(m4 - top), jnp.exp2(m16 - top)
    inv = 1.0 / (e1 * d1 + e4 * d4 + e16 * d16)

    def spread(w):
        hi = w.astype(BF16)
        lo = (w - hi.astype(F32)).astype(BF16)
        return _dot(jnp.concatenate([hi, lo], axis=1), e_ref[...])

    y = spread(e1 * inv) * o1 + spread(e4 * inv) * o4 + spread(e16 * inv) * o16
    ya = _rms(y, go_ref[...]).astype(BF16)
    out_ref[...] = (x_ref[...] + _dot(yf_ref[...].astype(BF16), w_ref[:FW]) + _dot(ya, w_ref[FW:FW + AW])
                    + _dot(ym_ref[...], w_ref[FW + AW:]))


def _unpermute_matrix(tm, dil):
    t = jnp.arange(tm, dtype=jnp.int32)
    return jax.nn.one_hot((t % dil) * (tm // dil) + t // dil, tm, dtype=BF16)


def _outproj(x, yf, o, lse, ym, go_a, w, l, *, tm=256):
    B, S, D = x.shape
    FW, AW, MW = yf.shape[-1], o[0].shape[-1], ym.shape[-1]
    assert S % tm == 0 and tm % 256 == 0 and all(t.shape[-1] == HEAD_DIM for t in lse)
    row = lambda w_: pl.BlockSpec((None, tm, w_), lambda b, i: (b, i, 0))
    cls = lambda d, w_: pl.BlockSpec((None, d, tm // d, w_), lambda b, i: (b, 0, i, 0))
    full = lambda a: pl.BlockSpec(a.shape, lambda b, i: (0,) * a.ndim)
    p4, p16 = _unpermute_matrix(tm, 4), _unpermute_matrix(tm, 16)
    spread = (jnp.arange(2 * HEAD_DIM, dtype=jnp.int32)[:, None] % HEAD_DIM
              == jnp.arange(AW, dtype=jnp.int32)[None, :] // HEAD_DIM).astype(BF16)
    return pl.pallas_call(
        functools.partial(_outproj_kernel, tm=tm, AW=AW),
        grid=(B, S // tm),
        in_specs=[row(D), row(FW), row(AW), cls(4, AW), cls(16, AW), row(HEAD_DIM), cls(4, HEAD_DIM),
                  cls(16, HEAD_DIM), row(MW), pl.BlockSpec((1, AW), lambda b, i: (0, 0)),
                  full(p4), full(p16), full(spread),
                  pl.BlockSpec((None,) + w.shape[1:], lambda b, i: (l, 0, 0))],
        out_specs=row(D),
        out_shape=jax.ShapeDtypeStruct((B, S, D), F32),
        scratch_shapes=[pltpu.VMEM((tm, HEAD_DIM), F32)] * 2,
        compiler_params=_params("parallel", "parallel"),
        name="outproj",
    )(x, yf, *o, *lse, ym, go_a.reshape(1, AW), p4, p16, spread, w)


def _rope_tables(S):
    inv = ROPE_THETA ** (-jnp.arange(0, HEAD_DIM, 2, dtype=F32) / HEAD_DIM)
    ang = jnp.arange(S, dtype=F32)[:, None] * inv[None, :]
    cos, sin = jnp.cos(ang), jnp.sin(ang)
    return jnp.concatenate([cos, cos], axis=1), jnp.concatenate([-sin, sin], axis=1)


def _trunk(x, mem, w, g, widths):
    B, S, D = x.shape
    FW, AW, MW = widths
    M = mem.shape[1]
    cos2, sin2 = _rope_tables(S)
    k1r, m2, cbd, sbd, tc, ts = _dft_tables(S, FW)
    for l in range(w["in"].shape[0]):
        x = _ffn(x.reshape(B * S, D), g["ffn1"][l], *w["ffn1"], l).reshape(B, S, D)

        f_in, *qkv, mq = _inproj(x, g["mix"][l], w["in"], l, widths, g["q"][l], g["k"][l], g["mq"][l],
                                 cos2, sin2)

        g_out = g["out"][l]
        y_f = _fft2(_fft1(f_in, k1r), tc, ts, m2, cbd, sbd, w["fourier"], l, g_out[:FW])

        outs, lses = [], []
        for i, dil in enumerate(DILATIONS):
            q, k, v = (t.reshape(B * dil, S // dil, AW) for t in qkv[3 * i:3 * i + 3])
            o, lse = _band_attention(q, k, v)
            lead = (B, S) if dil == 1 else (B, dil, S // dil)
            outs.append(o.reshape(*lead, AW))
            lses.append(lse.reshape(*lead, HEAD_DIM))

        mk, mv = _memkv(mem, g["mem"][l], w["mem_kv"], l, g["mk"][l])
        y_m = _memattn(mq, mk.reshape(B, M, MW), mv.reshape(B, M, MW), g_out[FW + AW:])

        x = _outproj(x, y_f, outs, lses, y_m, g_out[FW:FW + AW], w["out"], l)
        x = _ffn(x.reshape(B * S, D), g["ffn2"][l], *w["ffn2"], l).reshape(B, S, D)
    return x


def kernel(x_prompt, x_sample, mem_prompt, mem_sample, g_ffn1, w_ffn1_gate, w_ffn1_up, w_ffn1_down, g_mix, w_in, g_q, g_k, g_mem, w_mem_kv, g_mq, g_mk, w_fourier, g_out, w_out, g_ffn2, w_ffn2_gate, w_ffn2_up, w_ffn2_down):
    FW = w_fourier.shape[2]
    MW = w_mem_kv.shape[2] // 2
    AW = (w_in.shape[2] - FW - MW) // 3
    b16 = lambda a: a.astype(BF16)
    w = {"ffn1": (b16(w_ffn1_gate), b16(w_ffn1_up), b16(w_ffn1_down)),
         "ffn2": (b16(w_ffn2_gate), b16(w_ffn2_up), b16(w_ffn2_down)),
         "in": b16(w_in), "out": b16(w_out), "mem_kv": b16(w_mem_kv), "fourier": b16(w_fourier)}
    g = {"ffn1": g_ffn1, "ffn2": g_ffn2, "mix": g_mix, "q": g_q, "k": g_k, "mq": g_mq, "mk": g_mk,
         "mem": g_mem, "out": g_out}
    widths = (FW, AW, MW)
    return (_trunk(x_prompt, mem_prompt, w, g, widths), _trunk(x_sample, mem_sample, w, g, widths))
```

```python
import functools
import math

import jax
import jax.numpy as jnp
from jax import lax
from jax.experimental import pallas as pl
from jax.experimental.pallas import tpu as pltpu

EPS = 1e-6
HEAD_DIM = 128
ROPE_THETA = 10000.0
DILATIONS = (1, 4, 16)
HALF_WINDOW = 64
Q_BLOCK = 2 * HALF_WINDOW
K_WINDOW = 2 * Q_BLOCK
FFT_N2 = 64
FFT_ROWS = 8
Q_SCALE = HEAD_DIM ** -0.5 * math.log2(math.e)
HEAD_BLOCKS_PER_STEP = 32
STAT_SPLIT = HEAD_DIM // 2
INPROJ_SLABS = 8

BF16 = jnp.bfloat16
F32 = jnp.float32

VMEM_LIMIT_BYTES = 56 * 1024 * 1024


def _params(*semantics):
    return pltpu.CompilerParams(dimension_semantics=semantics, vmem_limit_bytes=VMEM_LIMIT_BYTES)


def _rms(x, g):
    return x * lax.rsqrt(jnp.mean(x * x, axis=-1, keepdims=True) + EPS) * g


def _dot(a, b):
    return jnp.dot(a, b, preferred_element_type=F32)


def _ffn_kernel(x_ref, g_ref, wg_ref, wu_ref, wd_ref, o_ref, h_ref):
    f = pl.program_id(1)

    @pl.when(f == 0)
    def _():
        x = x_ref[...]
        h_ref[...] = _rms(x, g_ref[...]).astype(BF16)
        o_ref[...] = x

    h = h_ref[...]
    gate = _dot(h, wg_ref[...])
    up = _dot(h, wu_ref[...])
    a = (0.5 * gate * (1.0 / (1.0 + jnp.exp(-gate))) * up).astype(BF16)
    o_ref[...] += _dot(a, wd_ref[...])


def _ffn(x, g, wg, wu, wd, l, *, tm=1024, tf=512):
    T, D = x.shape
    F = wg.shape[2]
    tf = min(tf, F)
    assert T % tm == 0 and F % tf == 0
    return pl.pallas_call(
        _ffn_kernel,
        grid=(T // tm, F // tf),
        in_specs=[
            pl.BlockSpec((tm, D), lambda i, f: (i, 0)),
            pl.BlockSpec((1, D), lambda i, f: (0, 0)),
            pl.BlockSpec((None, D, tf), lambda i, f: (l, 0, f)),
            pl.BlockSpec((None, D, tf), lambda i, f: (l, 0, f)),
            pl.BlockSpec((None, tf, D), lambda i, f: (l, f, 0)),
        ],
        out_specs=pl.BlockSpec((tm, D), lambda i, f: (i, 0)),
        out_shape=jax.ShapeDtypeStruct((T, D), F32),
        scratch_shapes=[pltpu.VMEM((tm, D), BF16)],
        compiler_params=_params("parallel", "arbitrary"),
        name="ffn",
    )(x, g.reshape(1, D), wg, wu, wd)


def _inproj_kernel(x_ref, g_ref, w_ref, gq_ref, gk_ref, gmq_ref, cos_ref, sin_ref,
                   f_ref, q1_ref, k1_ref, v1_ref, q4_ref, k4_ref, v4_ref, q16_ref, k16_ref, v16_ref, mq_ref,
                   nat_ref, cm4_ref, *, tm, widths, col_chunk):
    FW, AW, MW = widths
    h = _rms(x_ref[...], g_ref[...]).astype(BF16)
    cos = cos_ref[...]
    sin = sin_ref[...]
    slabs = nat_ref.shape[0]

    def heads(col0, width, emit, fn):
        cc = min(col_chunk, width)
        for c0 in range(0, width, cc):
            p = _dot(h, w_ref[:, col0 + c0:col0 + c0 + cc])
            for j in range(cc // HEAD_DIM):
                emit(c0 // HEAD_DIM + j, fn(p[:, j * HEAD_DIM:(j + 1) * HEAD_DIM]))

    def emit_f(head, t):
        f_ref[:, head * HEAD_DIM:(head + 1) * HEAD_DIM] = t

    def class_copies(base, out1_ref, out4_ref, out16_ref):
        def emit(head, t):
            sl = slice(head * HEAD_DIM, (head + 1) * HEAD_DIM)
            slab = (base + head) % slabs
            out1_ref[:, sl] = t.astype(BF16)
            nat_ref[slab] = t
            for r4 in range(4):
                c4 = nat_ref[slab, pl.ds(r4, tm // 4, stride=4), :]
                out4_ref[r4, :, sl] = c4.astype(BF16)
                cm4_ref[slab, r4 * (tm // 4):(r4 + 1) * (tm // 4), :] = c4
            for r16 in range(16):
                start = (r16 % 4) * (tm // 4) + r16 // 4
                c16 = cm4_ref[slab, pl.ds(start, tm // 16, stride=4), :]
                out16_ref[r16, :, sl] = c16.astype(BF16)
        return emit

    def norm_rope(g_ref_, post_scale):
        def fn(t):
            t = _rms(t, g_ref_[...])
            t = t * cos + pltpu.roll(t, HEAD_DIM // 2, 1) * sin
            return t if post_scale is None else t * post_scale
        return fn

    def emit_mq(head, t):
        mq_ref[:, head * HEAD_DIM:(head + 1) * HEAD_DIM] = t.astype(BF16)

    nh = AW // HEAD_DIM
    heads(0, FW, emit_f, lambda t: t)
    heads(FW, AW, class_copies(0, q1_ref, q4_ref, q16_ref), norm_rope(gq_ref, Q_SCALE))
    heads(FW + AW, AW, class_copies(nh, k1_ref, k4_ref, k16_ref), norm_rope(gk_ref, None))
    heads(FW + 3 * AW, MW, emit_mq, lambda t: _rms(t, gmq_ref[...]) * Q_SCALE)
    heads(FW + 2 * AW, AW, class_copies(2 * nh, v1_ref, v4_ref, v16_ref), lambda t: t)


def _inproj(x, g, w, l, widths, gq, gk, gmq, cos2, sin2, *, tm=512):
    B, S, D = x.shape
    FW, AW, MW = widths
    assert S % tm == 0 and tm % 256 == 0
    row = lambda w_: pl.BlockSpec((None, tm, w_), lambda b, i: (b, i, 0))
    cls = lambda d: pl.BlockSpec((None, d, tm // d, AW), lambda b, i: (b, 0, i, 0))
    vec = lambda n: pl.BlockSpec((1, n), lambda b, i: (0, 0))
    tab = pl.BlockSpec((tm, HEAD_DIM), lambda b, i: (i, 0))
    nat = jax.ShapeDtypeStruct((B, S, AW), BF16)
    c4 = jax.ShapeDtypeStruct((B, 4, S // 4, AW), BF16)
    c16 = jax.ShapeDtypeStruct((B, 16, S // 16, AW), BF16)
    return pl.pallas_call(
        functools.partial(_inproj_kernel, tm=tm, widths=widths, col_chunk=512),
        grid=(B, S // tm),
        in_specs=[row(D), vec(D),
                  pl.BlockSpec((None,) + w.shape[1:], lambda b, i: (l, 0, 0), pipeline_mode=pl.Buffered(1)),
                  vec(HEAD_DIM), vec(HEAD_DIM), vec(HEAD_DIM), tab, tab],
        out_specs=[row(FW), row(AW), row(AW), row(AW), cls(4), cls(4), cls(4), cls(16), cls(16), cls(16),
                   row(MW)],
        out_shape=[jax.ShapeDtypeStruct((B, S, FW), F32), nat, nat, nat, c4, c4, c4, c16, c16, c16,
                   jax.ShapeDtypeStruct((B, S, MW), BF16)],
        scratch_shapes=[pltpu.VMEM((INPROJ_SLABS, tm, HEAD_DIM), F32),
                        pltpu.VMEM((INPROJ_SLABS, tm, HEAD_DIM), F32)],
        compiler_params=_params("parallel", "parallel"),
        name="inproj",
    )(x, g.reshape(1, D), w, gq.reshape(1, -1), gk.reshape(1, -1), gmq.reshape(1, -1), cos2, sin2)


def _fft1_kernel(k_ref, x_ref, o_ref):
    n1, r, c = x_ref.shape
    y = _dot(k_ref[...], x_ref[...].reshape(n1 * r, c).astype(BF16))
    o_ref[...] = y.reshape(2, n1, r, c)


def _fft1(f_in, k1r):
    B, S, C = f_in.shape
    N2 = FFT_N2
    N1 = S // N2
    R = FFT_ROWS
    return pl.pallas_call(
        _fft1_kernel,
        grid=(B, N2 // R),
        in_specs=[pl.BlockSpec(k1r.shape, lambda b, j: (0, 0)),
                  pl.BlockSpec((None, N1, R, C), lambda b, j: (b, 0, j, 0))],
        out_specs=pl.BlockSpec((None, 2, N1, R, C), lambda b, j: (b, 0, 0, j, 0)),
        out_shape=jax.ShapeDtypeStruct((B, 2, N1, N2, C), F32),
        compiler_params=_params("parallel", "parallel"),
        name="fft1",
    )(k1r, f_in.reshape(B, N1, N2, C))


def _fft2_kernel(y_ref, tc_ref, ts_ref, m2_ref, p_ref, cbd_ref, sbd_ref, wf_ref, go_ref, o_ref, *, G, N2, scale):
    c = y_ref.shape[-1]
    reps = c // HEAD_DIM
    br, bi = y_ref[0], y_ref[1]
    tc = jnp.concatenate([tc_ref[...]] * reps, axis=1)
    ts = jnp.concatenate([ts_ref[...]] * reps, axis=1)
    pr = (br * tc + bi * ts).astype(BF16)
    pi = (bi * tc - br * ts).astype(BF16)
    blocks = lambda t: [t[g * N2:(g + 1) * N2] for g in range(G)]
    stacked = jnp.concatenate([jnp.concatenate(blocks(pr), axis=1), jnp.concatenate(blocks(pi), axis=1)], axis=0)
    z = _dot(m2_ref[...], stacked)
    rows = lambda t: jnp.concatenate([t[:, g * c:(g + 1) * c] for g in range(G)], axis=0)
    zr, zi = rows(z[:N2]).astype(BF16), rows(z[N2:]).astype(BF16)
    y = _dot(zr, cbd_ref[...]) + _dot(zi, sbd_ref[...])
    yf = _dot((y * scale).astype(BF16), wf_ref[...])
    o = _rms(yf, go_ref[...]).astype(BF16)
    o_ref[...] = _dot(p_ref[...], o).reshape(N2, G, c)


def _fft2(y1, tc, ts, m2, cbd, sbd, wf, l, go):
    B, _, N1, N2, C = y1.shape
    G = FFT_ROWS
    S = N1 * N2
    rows = G * N2
    full = lambda a: pl.BlockSpec(a.shape, lambda b, kb: (0,) * a.ndim)
    tw = pl.BlockSpec((rows, HEAD_DIM), lambda b, kb: (kb, 0))
    t = jnp.arange(rows, dtype=jnp.int32)
    perm = jax.nn.one_hot((t % G) * N2 + t // G, rows, dtype=BF16)
    out = pl.pallas_call(
        functools.partial(_fft2_kernel, G=G, N2=N2, scale=1.0 / math.sqrt(S * HEAD_DIM)),
        grid=(B, N1 // G),
        in_specs=[pl.BlockSpec((None, 2, rows, C), lambda b, kb: (b, 0, kb, 0)), tw, tw,
                  full(m2), full(perm), full(cbd), full(sbd),
                  pl.BlockSpec((None, C, C), lambda b, kb: (l, 0, 0)),
                  pl.BlockSpec((1, C), lambda b, kb: (0, 0))],
        out_specs=pl.BlockSpec((None, N2, G, C), lambda b, kb: (b, 0, kb, 0)),
        out_shape=jax.ShapeDtypeStruct((B, N2, N1, C), F32),
        compiler_params=_params("parallel", "parallel"),
        name="fft2",
    )(y1.reshape(B, 2, S, C), tc, ts, m2, perm, cbd, sbd, wf, go.reshape(1, C))
    return out.reshape(B, S, C)


def _dft_tables(S, C):
    N2 = FFT_N2
    N1 = S // N2
    R = FFT_ROWS

    def cs(n):
        j = jnp.arange(n, dtype=jnp.int32)
        ang = ((j[:, None] * j[None, :]) % n).astype(F32) * (2.0 * math.pi / n)
        return jnp.cos(ang), jnp.sin(ang)

    c1, s1 = cs(N1)
    f1 = jnp.concatenate([c1, -s1], axis=0)
    ri = jnp.arange(2 * N1 * R, dtype=jnp.int32)[:, None]
    ci = jnp.arange(N1 * R, dtype=jnp.int32)[None, :]
    pick_row = (ri // R == jnp.arange(2 * N1, dtype=jnp.int32)[None, :]).astype(F32)
    pick_col = (jnp.arange(N1, dtype=jnp.int32)[:, None] == ci // R).astype(F32)
    spread = jnp.dot(jnp.dot(pick_row, f1, precision=lax.Precision.HIGHEST), pick_col,
                     precision=lax.Precision.HIGHEST)
    k1r = jnp.where(ri % R == ci % R, spread, 0.0).astype(BF16)
    c2, s2 = cs(N2)
    m2 = jnp.concatenate([jnp.concatenate([c2, s2], axis=1),
                          jnp.concatenate([-s2, c2], axis=1)], axis=0).astype(BF16)
    cc, sc = cs(HEAD_DIM)
    eye_g = jnp.eye(C // HEAD_DIM, dtype=F32)
    cbd = jnp.kron(eye_g, cc).astype(BF16)
    sbd = jnp.kron(eye_g, sc).astype(BF16)
    k1 = jnp.arange(N1, dtype=jnp.int32)[:, None]
    n2 = jnp.arange(N2, dtype=jnp.int32)[None, :]
    ang = (k1 * n2).astype(F32).reshape(S, 1) * (2.0 * math.pi / S)
    tc = jnp.broadcast_to(jnp.cos(ang), (S, HEAD_DIM))
    ts = jnp.broadcast_to(jnp.sin(ang), (S, HEAD_DIM))
    return k1r, m2, cbd, sbd, tc, ts


def _band_kernel(*refs, L, heads, qb, seqs, halo):
    if halo:
        q_ref, k_ref, v_ref, kp_ref, vp_ref, kn_ref, vn_ref, o_ref, st_ref = refs
    else:
        q_ref, k_ref, v_ref, o_ref, st_ref = refs
    step = pl.program_id(1)
    ones = jnp.ones((K_WINDOW, HEAD_DIM), BF16)
    lane = lax.broadcasted_iota(jnp.int32, (Q_BLOCK, HEAD_DIM), 1)
    col = lax.broadcasted_iota(jnp.int32, (Q_BLOCK, K_WINDOW), 1)
    row_minus_col = lax.broadcasted_iota(jnp.int32, (Q_BLOCK, K_WINDOW), 0) - col

    def window(main_ref, prev_ref, next_ref, b, sl):
        lo, hi = b * Q_BLOCK - HALF_WINDOW, b * Q_BLOCK + K_WINDOW - HALF_WINDOW
        parts = []
        if lo < 0:
            parts.append(prev_ref[:, sl])
        parts.append(main_ref[max(lo, 0):min(hi, qb * Q_BLOCK), sl])
        if hi > qb * Q_BLOCK:
            parts.append(next_ref[:, sl])
        return parts[0] if len(parts) == 1 else jnp.concatenate(parts, axis=0)

    for s in range(seqs):
        for b in range(qb):
            rows = slice(b * Q_BLOCK, (b + 1) * Q_BLOCK)
            if halo:
                key = (step * qb + b) * Q_BLOCK - HALF_WINDOW + col
                ok = (jnp.abs(row_minus_col + HALF_WINDOW) <= HALF_WINDOW) & (key >= 0) & (key < L)
                q_s, o_s, st_s = q_ref, o_ref, st_ref
            else:
                ws = min(max(b * Q_BLOCK - HALF_WINDOW, 0), L - K_WINDOW)
                ok = jnp.abs(row_minus_col + (b * Q_BLOCK - ws)) <= HALF_WINDOW
                q_s, o_s, st_s = q_ref.at[s], o_ref.at[s], st_ref.at[s]
            bias = jnp.where(ok, 0.0, -jnp.inf)
            stats = jnp.where(lane < STAT_SPLIT, 0.0, 1.0)
            for h in range(heads):
                sl = slice(h * HEAD_DIM, (h + 1) * HEAD_DIM)
                if halo:
                    k = window(k_ref, kp_ref, kn_ref, b, sl)
                    v = window(v_ref, vp_ref, vn_ref, b, sl)
                else:
                    k = k_ref[s, ws:ws + K_WINDOW, sl]
                    v = v_ref[s, ws:ws + K_WINDOW, sl]
                sc = lax.dot_general(q_s[rows, sl], k, (((1,), (1,)), ((), ())),
                                     preferred_element_type=F32) + bias
                m = jnp.max(sc, axis=-1, keepdims=True)
                p = jnp.exp2(sc - m).astype(BF16)
                oa = _dot(p, jnp.concatenate([v, ones], axis=1))
                o_s[rows, sl] = oa[:, :HEAD_DIM].astype(BF16)
                stats = jnp.where(lane == h, m, jnp.where(lane == STAT_SPLIT + h, oa[:, HEAD_DIM:], stats))
            st_s[rows, :] = stats


def _band_attention(q, k, v):
    G, L, AW = q.shape
    heads = AW // HEAD_DIM
    nblk = L // Q_BLOCK
    assert L % Q_BLOCK == 0 and L >= K_WINDOW and heads <= STAT_SPLIT
    qb = max(1, min(HEAD_BLOCKS_PER_STEP // heads, nblk))
    assert nblk % qb == 0
    out_shape = [jax.ShapeDtypeStruct((G, L, AW), BF16), jax.ShapeDtypeStruct((G, L, HEAD_DIM), F32)]
    if qb == nblk:
        seqs = max(1, min(HEAD_BLOCKS_PER_STEP // (heads * qb), G))
        assert G % seqs == 0
        blk = lambda w: pl.BlockSpec((seqs, L, w), lambda g, n: (g, 0, 0))
        return pl.pallas_call(
            functools.partial(_band_kernel, L=L, heads=heads, qb=qb, seqs=seqs, halo=False),
            grid=(G // seqs, 1),
            in_specs=[blk(AW)] * 3,
            out_specs=[blk(AW), blk(HEAD_DIM)],
            out_shape=out_shape,
            compiler_params=_params("parallel", "parallel"),
            name="band_attention",
        )(q, k, v)
    main = pl.BlockSpec((None, qb * Q_BLOCK, AW), lambda g, n: (g, n, 0))
    halo_per_step = qb * Q_BLOCK // HALF_WINDOW
    prev = pl.BlockSpec((None, HALF_WINDOW, AW), lambda g, n: (g, jnp.maximum(n * halo_per_step - 1, 0), 0))
    nxt = pl.BlockSpec((None, HALF_WINDOW, AW),
                       lambda g, n: (g, jnp.minimum((n + 1) * halo_per_step, L // HALF_WINDOW - 1), 0))
    return pl.pallas_call(
        functools.partial(_band_kernel, L=L, heads=heads, qb=qb, seqs=1, halo=True),
        grid=(G, nblk // qb),
        in_specs=[main, main, main, prev, prev, nxt, nxt],
        out_specs=[main, pl.BlockSpec((None, qb * Q_BLOCK, HEAD_DIM), lambda g, n: (g, n, 0))],
        out_shape=out_shape,
        compiler_params=_params("parallel", "parallel"),
        name="band_attention",
    )(q, k, v, k, v, k, v)


def _memkv_kernel(m_ref, g_ref, w_ref, gk_ref, k_ref, v_ref, *, MW):
    h = _rms(m_ref[...], g_ref[...]).astype(BF16)
    kv = _dot(h, w_ref[...])
    for j in range(MW // HEAD_DIM):
        sl = slice(j * HEAD_DIM, (j + 1) * HEAD_DIM)
        k_ref[:, sl] = _rms(kv[:, sl], gk_ref[...]).astype(BF16)
    v_ref[...] = kv[:, MW:].astype(BF16)


def _memkv(mem, g, w, l, gk, *, tm=256):
    B, M, D = mem.shape
    MW = w.shape[2] // 2
    rows = B * M
    tm = min(tm, rows)
    assert rows % tm == 0
    out = pl.BlockSpec((tm, MW), lambda i: (i, 0))
    return pl.pallas_call(
        functools.partial(_memkv_kernel, MW=MW),
        grid=(rows // tm,),
        in_specs=[pl.BlockSpec((tm, D), lambda i: (i, 0)),
                  pl.BlockSpec((1, D), lambda i: (0, 0)),
                  pl.BlockSpec((None,) + w.shape[1:], lambda i: (l, 0, 0)),
                  pl.BlockSpec((1, HEAD_DIM), lambda i: (0, 0))],
        out_specs=[out, out],
        out_shape=[jax.ShapeDtypeStruct((rows, MW), BF16)] * 2,
        compiler_params=_params("parallel"),
        name="memkv",
    )(mem.reshape(rows, D), g.reshape(1, D), w, gk.reshape(1, -1))


def _memory_attention(q_ref, k_ref, v_ref, go_ref):
    ones = jnp.ones((k_ref.shape[0], HEAD_DIM), BF16)
    parts = []
    for j in range(q_ref.shape[1] // HEAD_DIM):
        sl = slice(j * HEAD_DIM, (j + 1) * HEAD_DIM)
        s = lax.dot_general(q_ref[:, sl], k_ref[:, sl], (((1,), (1,)), ((), ())), preferred_element_type=F32)
        p = jnp.exp2(s - jnp.max(s, axis=-1, keepdims=True)).astype(BF16)
        oa = _dot(p, jnp.concatenate([v_ref[:, sl], ones], axis=1))
        parts.append(oa[:, :HEAD_DIM] * (1.0 / oa[:, HEAD_DIM:]))
    return _rms(jnp.concatenate(parts, axis=1), go_ref[...]).astype(BF16)


def _outproj_kernel(x_ref, yf_ref, o1_ref, o4_ref, o16_ref, l1_ref, l4_ref, l16_ref, mq_ref, mk_ref, mv_ref,
                    go_ref, gom_ref, p4_ref, p16_ref, e_ref, w_ref, out_ref, s4_ref, s16_ref, *, tm, AW):
    FW = yf_ref.shape[-1]
    ym = _memory_attention(mq_ref, mk_ref, mv_ref, gom_ref)
    for dil, src, dst in ((4, l4_ref, s4_ref), (16, l16_ref, s16_ref)):
        for r in range(dil):
            dst[pl.ds(r, tm // dil, stride=dil), :] = src[r]
    o1 = o1_ref[...]
    o4 = _dot(p4_ref[...], o4_ref[...].reshape(tm, AW))
    o16 = _dot(p16_ref[...], o16_ref[...].reshape(tm, AW))

    low = lax.broadcasted_iota(jnp.int32, (tm, HEAD_DIM), 1) < STAT_SPLIT

    def split(st):
        return jnp.where(low, st, 0.0), jnp.where(low, pltpu.roll(st, STAT_SPLIT, 1), 1.0)

    (m1, d1), (m4, d4), (m16, d16) = split(l1_ref[...]), split(s4_ref[...]), split(s16_ref[...])
    top = jnp.maximum(jnp.maximum(m1, m4), m16)
    e1, e4, e16 = jnp.exp2(m1 - top), jnp.exp2(m4 - top), jnp.exp2(m16 - top)
    inv = 1.0 / (e1 * d1 + e4 * d4 + e16 * d16)

    def spread(w):
        hi = w.astype(BF16)
        lo = (w - hi.astype(F32)).astype(BF16)
        return _dot(jnp.concatenate([hi, lo], axis=1), e_ref[...])

    y = spread(e1 * inv) * o1 + spread(e4 * inv) * o4 + spread(e16 * inv) * o16
    ya = _rms(y, go_ref[...]).astype(BF16)
    out_ref[...] = (x_ref[...] + _dot(yf_ref[...].astype(BF16), w_ref[:FW]) + _dot(ya, w_ref[FW:FW + AW])
                    + _dot(ym, w_ref[FW + AW:]))


def _unpermute_matrix(tm, dil):
    t = jnp.arange(tm, dtype=jnp.int32)
    return jax.nn.one_hot((t % dil) * (tm // dil) + t // dil, tm, dtype=BF16)


def _outproj(x, yf, o, lse, mq, mk, mv, go_a, go_m, w, l, *, tm=256):
    B, S, D = x.shape
    FW, AW, MW = yf.shape[-1], o[0].shape[-1], mq.shape[-1]
    M = mk.shape[1]
    assert S % tm == 0 and tm % 256 == 0 and all(t.shape[-1] == HEAD_DIM for t in lse)
    row = lambda w_: pl.BlockSpec((None, tm, w_), lambda b, i: (b, i, 0))
    cls = lambda d, w_: pl.BlockSpec((None, d, tm // d, w_), lambda b, i: (b, 0, i, 0))
    full = lambda a: pl.BlockSpec(a.shape, lambda b, i: (0,) * a.ndim)
    mem = pl.BlockSpec((None, M, MW), lambda b, i: (b, 0, 0))
    p4, p16 = _unpermute_matrix(tm, 4), _unpermute_matrix(tm, 16)
    spread = (jnp.arange(2 * HEAD_DIM, dtype=jnp.int32)[:, None] % HEAD_DIM
              == jnp.arange(AW, dtype=jnp.int32)[None, :] // HEAD_DIM).astype(BF16)
    return pl.pallas_call(
        functools.partial(_outproj_kernel, tm=tm, AW=AW),
        grid=(B, S // tm),
        in_specs=[row(D), row(FW), row(AW), cls(4, AW), cls(16, AW), row(HEAD_DIM), cls(4, HEAD_DIM),
                  cls(16, HEAD_DIM), row(MW), mem, mem, pl.BlockSpec((1, AW), lambda b, i: (0, 0)),
                  pl.BlockSpec((1, MW), lambda b, i: (0, 0)), full(p4), full(p16), full(spread),
                  pl.BlockSpec((None,) + w.shape[1:], lambda b, i: (l, 0, 0))],
        out_specs=row(D),
        out_shape=jax.ShapeDtypeStruct((B, S, D), F32),
        scratch_shapes=[pltpu.VMEM((tm, HEAD_DIM), F32)] * 2,
        compiler_params=_params("parallel", "parallel"),
        name="outproj",
    )(x, yf, *o, *lse, mq, mk, mv, go_a.reshape(1, AW), go_m.reshape(1, MW), p4, p16, spread, w)


def _rope_tables(S):
    inv = ROPE_THETA ** (-jnp.arange(0, HEAD_DIM, 2, dtype=F32) / HEAD_DIM)
    ang = jnp.arange(S, dtype=F32)[:, None] * inv[None, :]
    cos, sin = jnp.cos(ang), jnp.sin(ang)
    return jnp.concatenate([cos, cos], axis=1), jnp.concatenate([-sin, sin], axis=1)


def _trunk(x, mem, w, g, widths):
    B, S, D = x.shape
    FW, AW, MW = widths
    M = mem.shape[1]
    cos2, sin2 = _rope_tables(S)
    k1r, m2, cbd, sbd, tc, ts = _dft_tables(S, FW)
    for l in range(w["in"].shape[0]):
        x = _ffn(x.reshape(B * S, D), g["ffn1"][l], *w["ffn1"], l).reshape(B, S, D)

        f_in, *qkv, mq = _inproj(x, g["mix"][l], w["in"], l, widths, g["q"][l], g["k"][l], g["mq"][l],
                                 cos2, sin2)

        g_out = g["out"][l]
        y_f = _fft2(_fft1(f_in, k1r), tc, ts, m2, cbd, sbd, w["fourier"], l, g_out[:FW])

        outs, lses = [], []
        for i, dil in enumerate(DILATIONS):
            q, k, v = (t.reshape(B * dil, S // dil, AW) for t in qkv[3 * i:3 * i + 3])
            o, lse = _band_attention(q, k, v)
            lead = (B, S) if dil == 1 else (B, dil, S // dil)
            outs.append(o.reshape(*lead, AW))
            lses.append(lse.reshape(*lead, HEAD_DIM))

        mk, mv = _memkv(mem, g["mem"][l], w["mem_kv"], l, g["mk"][l])
        x = _outproj(x, y_f, outs, lses, mq, mk.reshape(B, M, MW), mv.reshape(B, M, MW),
                     g_out[FW:FW + AW], g_out[FW + AW:], w["out"], l)
        x = _ffn(x.reshape(B * S, D), g["ffn2"][l], *w["ffn2"], l).reshape(B, S, D)
    return x


def kernel(x_prompt, x_sample, mem_prompt, mem_sample, g_ffn1, w_ffn1_gate, w_ffn1_up, w_ffn1_down, g_mix, w_in, g_q, g_k, g_mem, w_mem_kv, g_mq, g_mk, w_fourier, g_out, w_out, g_ffn2, w_ffn2_gate, w_ffn2_up, w_ffn2_down):
    FW = w_fourier.shape[2]
    MW = w_mem_kv.shape[2] // 2
    AW = (w_in.shape[2] - FW - MW) // 3
    b16 = lambda a: a.astype(BF16)
    w = {"ffn1": (b16(w_ffn1_gate), b16(w_ffn1_up), b16(w_ffn1_down)),
         "ffn2": (b16(w_ffn2_gate), b16(w_ffn2_up), b16(w_ffn2_down)),
         "in": b16(w_in), "out": b16(w_out), "mem_kv": b16(w_mem_kv), "fourier": b16(w_fourier)}
    g = {"ffn1": g_ffn1, "ffn2": g_ffn2, "mix": g_mix, "q": g_q, "k": g_k, "mq": g_mq, "mk": g_mk,
         "mem": g_mem, "out": g_out}
    widths = (FW, AW, MW)
    return (_trunk(x_prompt, mem_prompt, w, g, widths), _trunk(x_sample, mem_sample, w, g, widths))
```

```python
import functools
import math

import jax
import jax.numpy as jnp
from jax import lax
from jax.experimental import pallas as pl
from jax.experimental.pallas import tpu as pltpu

EPS = 1e-6
HEAD_DIM = 128
ROPE_THETA = 10000.0
DILATIONS = (1, 4, 16)
HALF_WINDOW = 64
Q_BLOCK = 2 * HALF_WINDOW
K_WINDOW = 2 * Q_BLOCK
FFT_N2 = 64
FFT_ROWS = 8
Q_SCALE = HEAD_DIM ** -0.5 * math.log2(math.e)
HEAD_BLOCKS_PER_STEP = 64
STAT_SPLIT = HEAD_DIM // 2
INPROJ_SLABS = 8

BF16 = jnp.bfloat16
F32 = jnp.float32

VMEM_LIMIT_BYTES = 56 * 1024 * 1024


def _params(*semantics):
    return pltpu.CompilerParams(dimension_semantics=semantics, vmem_limit_bytes=VMEM_LIMIT_BYTES)


def _rms(x, g):
    return x * lax.rsqrt(jnp.mean(x * x, axis=-1, keepdims=True) + EPS) * g


def _dot(a, b):
    return jnp.dot(a, b, preferred_element_type=F32)


def _ffn_kernel(x_ref, g_ref, wg_ref, wu_ref, wd_ref, o_ref, h_ref):
    f = pl.program_id(1)

    @pl.when(f == 0)
    def _():
        x = x_ref[...]
        h_ref[...] = _rms(x, g_ref[...]).astype(BF16)
        o_ref[...] = x

    h = h_ref[...]
    gate = _dot(h, wg_ref[...])
    up = _dot(h, wu_ref[...])
    a = (0.5 * gate * (1.0 / (1.0 + jnp.exp(-gate))) * up).astype(BF16)
    o_ref[...] += _dot(a, wd_ref[...])


def _ffn(x, g, wg, wu, wd, l, *, tm=1024, tf=512):
    T, D = x.shape
    F = wg.shape[2]
    tf = min(tf, F)
    assert T % tm == 0 and F % tf == 0
    return pl.pallas_call(
        _ffn_kernel,
        grid=(T // tm, F // tf),
        in_specs=[
            pl.BlockSpec((tm, D), lambda i, f: (i, 0)),
            pl.BlockSpec((1, D), lambda i, f: (0, 0)),
            pl.BlockSpec((None, D, tf), lambda i, f: (l, 0, f)),
            pl.BlockSpec((None, D, tf), lambda i, f: (l, 0, f)),
            pl.BlockSpec((None, tf, D), lambda i, f: (l, f, 0)),
        ],
        out_specs=pl.BlockSpec((tm, D), lambda i, f: (i, 0)),
        out_shape=jax.ShapeDtypeStruct((T, D), F32),
        scratch_shapes=[pltpu.VMEM((tm, D), BF16)],
        compiler_params=_params("parallel", "arbitrary"),
        name="ffn",
    )(x, g.reshape(1, D), wg, wu, wd)


def _inproj_kernel(x_ref, g_ref, w_ref, gq_ref, gk_ref, gmq_ref, cos_ref, sin_ref,
                   f_ref, q1_ref, k1_ref, v1_ref, q4_ref, k4_ref, v4_ref, q16_ref, k16_ref, v16_ref, mq_ref,
                   nat_ref, cm4_ref, *, tm, widths, col_chunk):
    FW, AW, MW = widths
    h = _rms(x_ref[...], g_ref[...]).astype(BF16)
    cos = cos_ref[...]
    sin = sin_ref[...]
    slabs = nat_ref.shape[0]

    def heads(col0, width, emit, fn):
        cc = min(col_chunk, width)
        for c0 in range(0, width, cc):
            p = _dot(h, w_ref[:, col0 + c0:col0 + c0 + cc])
            for j in range(cc // HEAD_DIM):
                emit(c0 // HEAD_DIM + j, fn(p[:, j * HEAD_DIM:(j + 1) * HEAD_DIM]))

    def emit_f(head, t):
        f_ref[:, head * HEAD_DIM:(head + 1) * HEAD_DIM] = t

    def class_copies(base, out1_ref, out4_ref, out16_ref):
        def emit(head, t):
            sl = slice(head * HEAD_DIM, (head + 1) * HEAD_DIM)
            slab = (base + head) % slabs
            out1_ref[:, sl] = t.astype(BF16)
            nat_ref[slab] = t
            for r4 in range(4):
                c4 = nat_ref[slab, pl.ds(r4, tm // 4, stride=4), :]
                out4_ref[r4, :, sl] = c4.astype(BF16)
                cm4_ref[slab, r4 * (tm // 4):(r4 + 1) * (tm // 4), :] = c4
            for r16 in range(16):
                start = (r16 % 4) * (tm // 4) + r16 // 4
                c16 = cm4_ref[slab, pl.ds(start, tm // 16, stride=4), :]
                out16_ref[r16, :, sl] = c16.astype(BF16)
        return emit

    def norm_rope(g_ref_, post_scale):
        def fn(t):
            t = _rms(t, g_ref_[...])
            t = t * cos + pltpu.roll(t, HEAD_DIM // 2, 1) * sin
            return t if post_scale is None else t * post_scale
        return fn

    def emit_mq(head, t):
        mq_ref[:, head * HEAD_DIM:(head + 1) * HEAD_DIM] = t.astype(BF16)

    nh = AW // HEAD_DIM
    heads(0, FW, emit_f, lambda t: t)
    heads(FW, AW, class_copies(0, q1_ref, q4_ref, q16_ref), norm_rope(gq_ref, Q_SCALE))
    heads(FW + AW, AW, class_copies(nh, k1_ref, k4_ref, k16_ref), norm_rope(gk_ref, None))
    heads(FW + 3 * AW, MW, emit_mq, lambda t: _rms(t, gmq_ref[...]) * Q_SCALE)
    heads(FW + 2 * AW, AW, class_copies(2 * nh, v1_ref, v4_ref, v16_ref), lambda t: t)


def _inproj(x, g, w, l, widths, gq, gk, gmq, cos2, sin2, *, tm=512):
    B, S, D = x.shape
    FW, AW, MW = widths
    assert S % tm == 0 and tm % 256 == 0
    row = lambda w_: pl.BlockSpec((None, tm, w_), lambda b, i: (b, i, 0))
    cls = lambda d: pl.BlockSpec((None, d, tm // d, AW), lambda b, i: (b, 0, i, 0))
    vec = lambda n: pl.BlockSpec((1, n), lambda b, i: (0, 0))
    tab = pl.BlockSpec((tm, HEAD_DIM), lambda b, i: (i, 0))
    nat = jax.ShapeDtypeStruct((B, S, AW), BF16)
    c4 = jax.ShapeDtypeStruct((B, 4, S // 4, AW), BF16)
    c16 = jax.ShapeDtypeStruct((B, 16, S // 16, AW), BF16)
    return pl.pallas_call(
        functools.partial(_inproj_kernel, tm=tm, widths=widths, col_chunk=512),
        grid=(B, S // tm),
        in_specs=[row(D), vec(D),
                  pl.BlockSpec((None,) + w.shape[1:], lambda b, i: (l, 0, 0), pipeline_mode=pl.Buffered(1)),
                  vec(HEAD_DIM), vec(HEAD_DIM), vec(HEAD_DIM), tab, tab],
        out_specs=[row(FW), row(AW), row(AW), row(AW), cls(4), cls(4), cls(4), cls(16), cls(16), cls(16),
                   row(MW)],
        out_shape=[jax.ShapeDtypeStruct((B, S, FW), F32), nat, nat, nat, c4, c4, c4, c16, c16, c16,
                   jax.ShapeDtypeStruct((B, S, MW), BF16)],
        scratch_shapes=[pltpu.VMEM((INPROJ_SLABS, tm, HEAD_DIM), F32),
                        pltpu.VMEM((INPROJ_SLABS, tm, HEAD_DIM), F32)],
        compiler_params=_params("parallel", "parallel"),
        name="inproj",
    )(x, g.reshape(1, D), w, gq.reshape(1, -1), gk.reshape(1, -1), gmq.reshape(1, -1), cos2, sin2)


def _fft1_kernel(k_ref, x_ref, o_ref):
    n1, r, c = x_ref.shape
    y = _dot(k_ref[...], x_ref[...].reshape(n1 * r, c).astype(BF16))
    o_ref[...] = y.reshape(2, n1, r, c)


def _fft1(f_in, k1r):
    B, S, C = f_in.shape
    N2 = FFT_N2
    N1 = S // N2
    R = FFT_ROWS
    return pl.pallas_call(
        _fft1_kernel,
        grid=(B, N2 // R),
        in_specs=[pl.BlockSpec(k1r.shape, lambda b, j: (0, 0)),
                  pl.BlockSpec((None, N1, R, C), lambda b, j: (b, 0, j, 0))],
        out_specs=pl.BlockSpec((None, 2, N1, R, C), lambda b, j: (b, 0, 0, j, 0)),
        out_shape=jax.ShapeDtypeStruct((B, 2, N1, N2, C), F32),
        compiler_params=_params("parallel", "parallel"),
        name="fft1",
    )(k1r, f_in.reshape(B, N1, N2, C))


def _fft2_kernel(y_ref, tc_ref, ts_ref, m2_ref, p_ref, cbd_ref, sbd_ref, wf_ref, go_ref, o_ref, *, G, N2, scale):
    c = y_ref.shape[-1]
    reps = c // HEAD_DIM
    br, bi = y_ref[0], y_ref[1]
    tc = jnp.concatenate([tc_ref[...]] * reps, axis=1)
    ts = jnp.concatenate([ts_ref[...]] * reps, axis=1)
    pr = (br * tc + bi * ts).astype(BF16)
    pi = (bi * tc - br * ts).astype(BF16)
    blocks = lambda t: [t[g * N2:(g + 1) * N2] for g in range(G)]
    stacked = jnp.concatenate([jnp.concatenate(blocks(pr), axis=1), jnp.concatenate(blocks(pi), axis=1)], axis=0)
    z = _dot(m2_ref[...], stacked)
    rows = lambda t: jnp.concatenate([t[:, g * c:(g + 1) * c] for g in range(G)], axis=0)
    zr, zi = rows(z[:N2]).astype(BF16), rows(z[N2:]).astype(BF16)
    y = _dot(zr, cbd_ref[...]) + _dot(zi, sbd_ref[...])
    yf = _dot((y * scale).astype(BF16), wf_ref[...])
    o = _rms(yf, go_ref[...]).astype(BF16)
    o_ref[...] = _dot(p_ref[...], o).reshape(N2, G, c)


def _fft2(y1, tc, ts, m2, cbd, sbd, wf, l, go):
    B, _, N1, N2, C = y1.shape
    G = FFT_ROWS
    S = N1 * N2
    rows = G * N2
    full = lambda a: pl.BlockSpec(a.shape, lambda b, kb: (0,) * a.ndim)
    tw = pl.BlockSpec((rows, HEAD_DIM), lambda b, kb: (kb, 0))
    t = jnp.arange(rows, dtype=jnp.int32)
    perm = jax.nn.one_hot((t % G) * N2 + t // G, rows, dtype=BF16)
    out = pl.pallas_call(
        functools.partial(_fft2_kernel, G=G, N2=N2, scale=1.0 / math.sqrt(S * HEAD_DIM)),
        grid=(B, N1 // G),
        in_specs=[pl.BlockSpec((None, 2, rows, C), lambda b, kb: (b, 0, kb, 0)), tw, tw,
                  full(m2), full(perm), full(cbd), full(sbd),
                  pl.BlockSpec((None, C, C), lambda b, kb: (l, 0, 0)),
                  pl.BlockSpec((1, C), lambda b, kb: (0, 0))],
        out_specs=pl.BlockSpec((None, N2, G, C), lambda b, kb: (b, 0, kb, 0)),
        out_shape=jax.ShapeDtypeStruct((B, N2, N1, C), F32),
        compiler_params=_params("parallel", "parallel"),
        name="fft2",
    )(y1.reshape(B, 2, S, C), tc, ts, m2, perm, cbd, sbd, wf, go.reshape(1, C))
    return out.reshape(B, S, C)


def _dft_tables(S, C):
    N2 = FFT_N2
    N1 = S // N2
    R = FFT_ROWS

    def cs(n):
        j = jnp.arange(n, dtype=jnp.int32)
        ang = ((j[:, None] * j[None, :]) % n).astype(F32) * (2.0 * math.pi / n)
        return jnp.cos(ang), jnp.sin(ang)

    c1, s1 = cs(N1)
    f1 = jnp.concatenate([c1, -s1], axis=0)
    ri = jnp.arange(2 * N1 * R, dtype=jnp.int32)[:, None]
    ci = jnp.arange(N1 * R, dtype=jnp.int32)[None, :]
    pick_row = (ri // R == jnp.arange(2 * N1, dtype=jnp.int32)[None, :]).astype(F32)
    pick_col = (jnp.arange(N1, dtype=jnp.int32)[:, None] == ci // R).astype(F32)
    spread = jnp.dot(jnp.dot(pick_row, f1, precision=lax.Precision.HIGHEST), pick_col,
                     precision=lax.Precision.HIGHEST)
    k1r = jnp.where(ri % R == ci % R, spread, 0.0).astype(BF16)
    c2, s2 = cs(N2)
    m2 = jnp.concatenate([jnp.concatenate([c2, s2], axis=1),
                          jnp.concatenate([-s2, c2], axis=1)], axis=0).astype(BF16)
    cc, sc = cs(HEAD_DIM)
    eye_g = jnp.eye(C // HEAD_DIM, dtype=F32)
    cbd = jnp.kron(eye_g, cc).astype(BF16)
    sbd = jnp.kron(eye_g, sc).astype(BF16)
    k1 = jnp.arange(N1, dtype=jnp.int32)[:, None]
    n2 = jnp.arange(N2, dtype=jnp.int32)[None, :]
    ang = (k1 * n2).astype(F32).reshape(S, 1) * (2.0 * math.pi / S)
    tc = jnp.broadcast_to(jnp.cos(ang), (S, HEAD_DIM))
    ts = jnp.broadcast_to(jnp.sin(ang), (S, HEAD_DIM))
    return k1r, m2, cbd, sbd, tc, ts


def _band_kernel(*refs, L, heads, qb, seqs, halo):
    if halo:
        q_ref, k_ref, v_ref, kp_ref, vp_ref, kn_ref, vn_ref, o_ref, st_ref = refs
    else:
        q_ref, k_ref, v_ref, o_ref, st_ref = refs
    step = pl.program_id(1)
    ones = jnp.ones((K_WINDOW, HEAD_DIM), BF16)
    lane = lax.broadcasted_iota(jnp.int32, (Q_BLOCK, HEAD_DIM), 1)
    col = lax.broadcasted_iota(jnp.int32, (Q_BLOCK, K_WINDOW), 1)
    row_minus_col = lax.broadcasted_iota(jnp.int32, (Q_BLOCK, K_WINDOW), 0) - col

    def window(main_ref, prev_ref, next_ref, b, sl):
        lo, hi = b * Q_BLOCK - HALF_WINDOW, b * Q_BLOCK + K_WINDOW - HALF_WINDOW
        parts = []
        if lo < 0:
            parts.append(prev_ref[:, sl])
        parts.append(main_ref[max(lo, 0):min(hi, qb * Q_BLOCK), sl])
        if hi > qb * Q_BLOCK:
            parts.append(next_ref[:, sl])
        return parts[0] if len(parts) == 1 else jnp.concatenate(parts, axis=0)

    for s in range(seqs):
        for b in range(qb):
            rows = slice(b * Q_BLOCK, (b + 1) * Q_BLOCK)
            if halo:
                key = (step * qb + b) * Q_BLOCK - HALF_WINDOW + col
                ok = (jnp.abs(row_minus_col + HALF_WINDOW) <= HALF_WINDOW) & (key >= 0) & (key < L)
                q_s, o_s, st_s = q_ref, o_ref, st_ref
            else:
                ws = min(max(b * Q_BLOCK - HALF_WINDOW, 0), L - K_WINDOW)
                ok = jnp.abs(row_minus_col + (b * Q_BLOCK - ws)) <= HALF_WINDOW
                q_s, o_s, st_s = q_ref.at[s], o_ref.at[s], st_ref.at[s]
            bias = jnp.where(ok, 0.0, -jnp.inf)
            stats = jnp.where(lane < STAT_SPLIT, 0.0, 1.0)
            for h in range(heads):
                sl = slice(h * HEAD_DIM, (h + 1) * HEAD_DIM)
                if halo:
                    k = window(k_ref, kp_ref, kn_ref, b, sl)
                    v = window(v_ref, vp_ref, vn_ref, b, sl)
                else:
                    k = k_ref[s, ws:ws + K_WINDOW, sl]
                    v = v_ref[s, ws:ws + K_WINDOW, sl]
                sc = lax.dot_general(q_s[rows, sl], k, (((1,), (1,)), ((), ())),
                                     preferred_element_type=F32) + bias
                m = jnp.max(sc, axis=-1, keepdims=True)
                p = jnp.exp2(sc - m).astype(BF16)
                oa = _dot(p, jnp.concatenate([v, ones], axis=1))
                o_s[rows, sl] = oa[:, :HEAD_DIM].astype(BF16)
                stats = jnp.where(lane == h, m, jnp.where(lane == STAT_SPLIT + h, oa[:, HEAD_DIM:], stats))
            st_s[rows, :] = stats


def _band_attention(q, k, v):
    G, L, AW = q.shape
    heads = AW // HEAD_DIM
    nblk = L // Q_BLOCK
    assert L % Q_BLOCK == 0 and L >= K_WINDOW and heads <= STAT_SPLIT
    qb = max(1, min(HEAD_BLOCKS_PER_STEP // heads, nblk))
    assert nblk % qb == 0
    out_shape = [jax.ShapeDtypeStruct((G, L, AW), BF16), jax.ShapeDtypeStruct((G, L, HEAD_DIM), F32)]
    if qb == nblk:
        seqs = max(1, min(HEAD_BLOCKS_PER_STEP // (heads * qb), G))
        assert G % seqs == 0
        blk = lambda w: pl.BlockSpec((seqs, L, w), lambda g, n: (g, 0, 0))
        return pl.pallas_call(
            functools.partial(_band_kernel, L=L, heads=heads, qb=qb, seqs=seqs, halo=False),
            grid=(G // seqs, 1),
            in_specs=[blk(AW)] * 3,
            out_specs=[blk(AW), blk(HEAD_DIM)],
            out_shape=out_shape,
            compiler_params=_params("parallel", "parallel"),
            name="band_attention",
        )(q, k, v)
    main = pl.BlockSpec((None, qb * Q_BLOCK, AW), lambda g, n: (g, n, 0))
    halo_per_step = qb * Q_BLOCK // HALF_WINDOW
    prev = pl.BlockSpec((None, HALF_WINDOW, AW), lambda g, n: (g, jnp.maximum(n * halo_per_step - 1, 0), 0))
    nxt = pl.BlockSpec((None, HALF_WINDOW, AW),
                       lambda g, n: (g, jnp.minimum((n + 1) * halo_per_step, L // HALF_WINDOW - 1), 0))
    return pl.pallas_call(
        functools.partial(_band_kernel, L=L, heads=heads, qb=qb, seqs=1, halo=True),
        grid=(G, nblk // qb),
        in_specs=[main, main, main, prev, prev, nxt, nxt],
        out_specs=[main, pl.BlockSpec((None, qb * Q_BLOCK, HEAD_DIM), lambda g, n: (g, n, 0))],
        out_shape=out_shape,
        compiler_params=_params("parallel", "parallel"),
        name="band_attention",
    )(q, k, v, k, v, k, v)


def _memkv_kernel(m_ref, g_ref, w_ref, gk_ref, k_ref, v_ref, *, MW):
    h = _rms(m_ref[...], g_ref[...]).astype(BF16)
    kv = _dot(h, w_ref[...])
    for j in range(MW // HEAD_DIM):
        sl = slice(j * HEAD_DIM, (j + 1) * HEAD_DIM)
        k_ref[:, sl] = _rms(kv[:, sl], gk_ref[...]).astype(BF16)
    v_ref[...] = kv[:, MW:].astype(BF16)


def _memkv(mem, g, w, l, gk, *, tm=256):
    B, M, D = mem.shape
    MW = w.shape[2] // 2
    rows = B * M
    tm = min(tm, rows)
    assert rows % tm == 0
    out = pl.BlockSpec((tm, MW), lambda i: (i, 0))
    return pl.pallas_call(
        functools.partial(_memkv_kernel, MW=MW),
        grid=(rows // tm,),
        in_specs=[pl.BlockSpec((tm, D), lambda i: (i, 0)),
                  pl.BlockSpec((1, D), lambda i: (0, 0)),
                  pl.BlockSpec((None,) + w.shape[1:], lambda i: (l, 0, 0)),
                  pl.BlockSpec((1, HEAD_DIM), lambda i: (0, 0))],
        out_specs=[out, out],
        out_shape=[jax.ShapeDtypeStruct((rows, MW), BF16)] * 2,
        compiler_params=_params("parallel"),
        name="memkv",
    )(mem.reshape(rows, D), g.reshape(1, D), w, gk.reshape(1, -1))


def _memory_attention(q_ref, k_ref, v_ref, go_ref):
    ones = jnp.ones((k_ref.shape[0], HEAD_DIM), BF16)
    parts = []
    for j in range(q_ref.shape[1] // HEAD_DIM):
        sl = slice(j * HEAD_DIM, (j + 1) * HEAD_DIM)
        s = lax.dot_general(q_ref[:, sl], k_ref[:, sl], (((1,), (1,)), ((), ())), preferred_element_type=F32)
        p = jnp.exp2(s - jnp.max(s, axis=-1, keepdims=True)).astype(BF16)
        oa = _dot(p, jnp.concatenate([v_ref[:, sl], ones], axis=1))
        parts.append(oa[:, :HEAD_DIM] * (1.0 / oa[:, HEAD_DIM:]))
    return _rms(jnp.concatenate(parts, axis=1), go_ref[...]).astype(BF16)


def _outproj_kernel(x_ref, yf_ref, o1_ref, o4_ref, o16_ref, l1_ref, l4_ref, l16_ref, mq_ref, mk_ref, mv_ref,
                    go_ref, gom_ref, p4_ref, p16_ref, e_ref, w_ref, out_ref, s4_ref, s16_ref, *, tm, AW):
    FW = yf_ref.shape[-1]
    ym = _memory_attention(mq_ref, mk_ref, mv_ref, gom_ref)
    for dil, src, dst in ((4, l4_ref, s4_ref), (16, l16_ref, s16_ref)):
        for r in range(dil):
            dst[pl.ds(r, tm // dil, stride=dil), :] = src[r]
    o1 = o1_ref[...]
    o4 = _dot(p4_ref[...], o4_ref[...].reshape(tm, AW))
    o16 = _dot(p16_ref[...], o16_ref[...].reshape(tm, AW))

    low = lax.broadcasted_iota(jnp.int32, (tm, HEAD_DIM), 1) < STAT_SPLIT

    def split(st):
        return jnp.where(low, st, 0.0), jnp.where(low, pltpu.roll(st, STAT_SPLIT, 1), 1.0)

    (m1, d1), (m4, d4), (m16, d16) = split(l1_ref[...]), split(s4_ref[...]), split(s16_ref[...])
    top = jnp.maximum(jnp.maximum(m1, m4), m16)
    e1, e4, e16 = jnp.exp2(m1 - top), jnp.exp2(m4 - top), jnp.exp2(m16 - top)
    inv = 1.0 / (e1 * d1 + e4 * d4 + e16 * d16)

    def spread(w):
        hi = w.astype(BF16)
        lo = (w - hi.astype(F32)).astype(BF16)
        return _dot(jnp.concatenate([hi, lo], axis=1), e_ref[...])

    y = spread(e1 * inv) * o1 + spread(e4 * inv) * o4 + spread(e16 * inv) * o16
    ya = _rms(y, go_ref[...]).astype(BF16)
    out_ref[...] = (x_ref[...] + _dot(yf_ref[...].astype(BF16), w_ref[:FW]) + _dot(ya, w_ref[FW:FW + AW])
                    + _dot(ym, w_ref[FW + AW:]))


def _unpermute_matrix(tm, dil):
    t = jnp.arange(tm, dtype=jnp.int32)
    return jax.nn.one_hot((t % dil) * (tm // dil) + t // dil, tm, dtype=BF16)


def _outproj(x, yf, o, lse, mq, mk, mv, go_a, go_m, w, l, *, tm=256):
    B, S, D = x.shape
    FW, AW, MW = yf.shape[-1], o[0].shape[-1], mq.shape[-1]
    M = mk.shape[1]
    assert S % tm == 0 and tm % 256 == 0 and all(t.shape[-1] == HEAD_DIM for t in lse)
    row = lambda w_: pl.BlockSpec((None, tm, w_), lambda b, i: (b, i, 0))
    cls = lambda d, w_: pl.BlockSpec((None, d, tm // d, w_), lambda b, i: (b, 0, i, 0))
    full = lambda a: pl.BlockSpec(a.shape, lambda b, i: (0,) * a.ndim)
    mem = pl.BlockSpec((None, M, MW), lambda b, i: (b, 0, 0))
    p4, p16 = _unpermute_matrix(tm, 4), _unpermute_matrix(tm, 16)
    spread = (jnp.arange(2 * HEAD_DIM, dtype=jnp.int32)[:, None] % HEAD_DIM
              == jnp.arange(AW, dtype=jnp.int32)[None, :] // HEAD_DIM).astype(BF16)
    return pl.pallas_call(
        functools.partial(_outproj_kernel, tm=tm, AW=AW),
        grid=(B, S // tm),
        in_specs=[row(D), row(FW), row(AW), cls(4, AW), cls(16, AW), row(HEAD_DIM), cls(4, HEAD_DIM),
                  cls(16, HEAD_DIM), row(MW), mem, mem, pl.BlockSpec((1, AW), lambda b, i: (0, 0)),
                  pl.BlockSpec((1, MW), lambda b, i: (0, 0)), full(p4), full(p16), full(spread),
                  pl.BlockSpec((None,) + w.shape[1:], lambda b, i: (l, 0, 0))],
        out_specs=row(D),
        out_shape=jax.ShapeDtypeStruct((B, S, D), F32),
        scratch_shapes=[pltpu.VMEM((tm, HEAD_DIM), F32)] * 2,
        compiler_params=_params("parallel", "parallel"),
        name="outproj",
    )(x, yf, *o, *lse, mq, mk, mv, go_a.reshape(1, AW), go_m.reshape(1, MW), p4, p16, spread, w)


def _rope_tables(S):
    inv = ROPE_THETA ** (-jnp.arange(0, HEAD_DIM, 2, dtype=F32) / HEAD_DIM)
    ang = jnp.arange(S, dtype=F32)[:, None] * inv[None, :]
    cos, sin = jnp.cos(ang), jnp.sin(ang)
    return jnp.concatenate([cos, cos], axis=1), jnp.concatenate([-sin, sin], axis=1)


def _trunk(x, mem, w, g, widths):
    B, S, D = x.shape
    FW, AW, MW = widths
    M = mem.shape[1]
    cos2, sin2 = _rope_tables(S)
    k1r, m2, cbd, sbd, tc, ts = _dft_tables(S, FW)
    for l in range(w["in"].shape[0]):
        x = _ffn(x.reshape(B * S, D), g["ffn1"][l], *w["ffn1"], l).reshape(B, S, D)

        f_in, *qkv, mq = _inproj(x, g["mix"][l], w["in"], l, widths, g["q"][l], g["k"][l], g["mq"][l],
                                 cos2, sin2)

        g_out = g["out"][l]
        y_f = _fft2(_fft1(f_in, k1r), tc, ts, m2, cbd, sbd, w["fourier"], l, g_out[:FW])

        outs, lses = [], []
        for i, dil in enumerate(DILATIONS):
            q, k, v = (t.reshape(B * dil, S // dil, AW) for t in qkv[3 * i:3 * i + 3])
            o, lse = _band_attention(q, k, v)
            lead = (B, S) if dil == 1 else (B, dil, S // dil)
            outs.append(o.reshape(*lead, AW))
            lses.append(lse.reshape(*lead, HEAD_DIM))

        mk, mv = _memkv(mem, g["mem"][l], w["mem_kv"], l, g["mk"][l])
        x = _outproj(x, y_f, outs, lses, mq, mk.reshape(B, M, MW), mv.reshape(B, M, MW),
                     g_out[FW:FW + AW], g_out[FW + AW:], w["out"], l)
        x = _ffn(x.reshape(B * S, D), g["ffn2"][l], *w["ffn2"], l).reshape(B, S, D)
    return x


def kernel(x_prompt, x_sample, mem_prompt, mem_sample, g_ffn1, w_ffn1_gate, w_ffn1_up, w_ffn1_down, g_mix, w_in, g_q, g_k, g_mem, w_mem_kv, g_mq, g_mk, w_fourier, g_out, w_out, g_ffn2, w_ffn2_gate, w_ffn2_up, w_ffn2_down):
    FW = w_fourier.shape[2]
    MW = w_mem_kv.shape[2] // 2
    AW = (w_in.shape[2] - FW - MW) // 3
    b16 = lambda a: a.astype(BF16)
    w = {"ffn1": (b16(w_ffn1_gate), b16(w_ffn1_up), b16(w_ffn1_down)),
         "ffn2": (b16(w_ffn2_gate), b16(w_ffn2_up), b16(w_ffn2_down)),
         "in": b16(w_in), "out": b16(w_out), "mem_kv": b16(w_mem_kv), "fourier": b16(w_fourier)}
    g = {"ffn1": g_ffn1, "ffn2": g_ffn2, "mix": g_mix, "q": g_q, "k": g_k, "mq": g_mq, "mk": g_mk,
         "mem": g_mem, "out": g_out}
    widths = (FW, AW, MW)
    return (_trunk(x_prompt, mem_prompt, w, g, widths), _trunk(x_sample, mem_sample, w, g, widths))
```

```python
import functools
import math

import jax
import jax.numpy as jnp
from jax import lax
from jax.experimental import pallas as pl
from jax.experimental.pallas import tpu as pltpu

EPS = 1e-6
HEAD_DIM = 128
ROPE_THETA = 10000.0
DILATIONS = (1, 4, 16)
HALF_WINDOW = 64
Q_BLOCK = 2 * HALF_WINDOW
K_WINDOW = 2 * Q_BLOCK
FFT_N2 = 64
FFT_ROWS = 8
Q_SCALE = HEAD_DIM ** -0.5 * math.log2(math.e)
FFT1_ROWS_PER_STEP = 16
HEAD_BLOCKS_PER_STEP = 128
STAT_SPLIT = HEAD_DIM // 2
INPROJ_SLABS = 8

BF16 = jnp.bfloat16
F32 = jnp.float32

VMEM_LIMIT_BYTES = 56 * 1024 * 1024


def _params(*semantics):
    return pltpu.CompilerParams(dimension_semantics=semantics, vmem_limit_bytes=VMEM_LIMIT_BYTES)


def _rms(x, g):
    return x * lax.rsqrt(jnp.mean(x * x, axis=-1, keepdims=True) + EPS) * g


def _dot(a, b):
    return jnp.dot(a, b, preferred_element_type=F32)


def _ffn_kernel(x_ref, g_ref, wg_ref, wu_ref, wd_ref, o_ref, h_ref):
    f = pl.program_id(1)

    @pl.when(f == 0)
    def _():
        x = x_ref[...]
        h_ref[...] = _rms(x, g_ref[...]).astype(BF16)
        o_ref[...] = x

    h = h_ref[...]
    gate = _dot(h, wg_ref[...])
    up = _dot(h, wu_ref[...])
    a = (0.5 * gate * (1.0 / (1.0 + jnp.exp(-gate))) * up).astype(BF16)
    o_ref[...] += _dot(a, wd_ref[...])


def _ffn(x, g, wg, wu, wd, l, *, tm=1024, tf=512):
    T, D = x.shape
    F = wg.shape[2]
    tf = min(tf, F)
    assert T % tm == 0 and F % tf == 0
    return pl.pallas_call(
        _ffn_kernel,
        grid=(T // tm, F // tf),
        in_specs=[
            pl.BlockSpec((tm, D), lambda i, f: (i, 0)),
            pl.BlockSpec((1, D), lambda i, f: (0, 0)),
            pl.BlockSpec((None, D, tf), lambda i, f: (l, 0, f)),
            pl.BlockSpec((None, D, tf), lambda i, f: (l, 0, f)),
            pl.BlockSpec((None, tf, D), lambda i, f: (l, f, 0)),
        ],
        out_specs=pl.BlockSpec((tm, D), lambda i, f: (i, 0)),
        out_shape=jax.ShapeDtypeStruct((T, D), F32),
        scratch_shapes=[pltpu.VMEM((tm, D), BF16)],
        compiler_params=_params("parallel", "arbitrary"),
        name="ffn",
    )(x, g.reshape(1, D), wg, wu, wd)


def _inproj_kernel(x_ref, g_ref, w_ref, gq_ref, gk_ref, gmq_ref, cos_ref, sin_ref,
                   f_ref, q1_ref, k1_ref, v1_ref, q4_ref, k4_ref, v4_ref, q16_ref, k16_ref, v16_ref, mq_ref,
                   nat_ref, cm4_ref, *, tm, widths, col_chunk):
    FW, AW, MW = widths
    h = _rms(x_ref[...], g_ref[...]).astype(BF16)
    cos = cos_ref[...]
    sin = sin_ref[...]
    slabs = nat_ref.shape[0]

    def heads(col0, width, emit, fn):
        cc = min(col_chunk, width)
        for c0 in range(0, width, cc):
            p = _dot(h, w_ref[:, col0 + c0:col0 + c0 + cc])
            for j in range(cc // HEAD_DIM):
                emit(c0 // HEAD_DIM + j, fn(p[:, j * HEAD_DIM:(j + 1) * HEAD_DIM]))

    def emit_f(head, t):
        f_ref[:, head * HEAD_DIM:(head + 1) * HEAD_DIM] = t

    def class_copies(base, out1_ref, out4_ref, out16_ref):
        def emit(head, t):
            sl = slice(head * HEAD_DIM, (head + 1) * HEAD_DIM)
            slab = (base + head) % slabs
            out1_ref[:, sl] = t.astype(BF16)
            nat_ref[slab] = t
            for r4 in range(4):
                c4 = nat_ref[slab, pl.ds(r4, tm // 4, stride=4), :]
                out4_ref[r4, :, sl] = c4.astype(BF16)
                cm4_ref[slab, r4 * (tm // 4):(r4 + 1) * (tm // 4), :] = c4
            for r16 in range(16):
                start = (r16 % 4) * (tm // 4) + r16 // 4
                c16 = cm4_ref[slab, pl.ds(start, tm // 16, stride=4), :]
                out16_ref[r16, :, sl] = c16.astype(BF16)
        return emit

    def norm_rope(g_ref_, post_scale):
        def fn(t):
            t = _rms(t, g_ref_[...])
            t = t * cos + pltpu.roll(t, HEAD_DIM // 2, 1) * sin
            return t if post_scale is None else t * post_scale
        return fn

    def emit_mq(head, t):
        mq_ref[:, head * HEAD_DIM:(head + 1) * HEAD_DIM] = t.astype(BF16)

    nh = AW // HEAD_DIM
    heads(0, FW, emit_f, lambda t: t)
    heads(FW, AW, class_copies(0, q1_ref, q4_ref, q16_ref), norm_rope(gq_ref, Q_SCALE))
    heads(FW + AW, AW, class_copies(nh, k1_ref, k4_ref, k16_ref), norm_rope(gk_ref, None))
    heads(FW + 3 * AW, MW, emit_mq, lambda t: _rms(t, gmq_ref[...]) * Q_SCALE)
    heads(FW + 2 * AW, AW, class_copies(2 * nh, v1_ref, v4_ref, v16_ref), lambda t: t)


def _inproj(x, g, w, l, widths, gq, gk, gmq, cos2, sin2, *, tm=512):
    B, S, D = x.shape
    FW, AW, MW = widths
    assert S % tm == 0 and tm % 256 == 0
    row = lambda w_: pl.BlockSpec((None, tm, w_), lambda b, i: (b, i, 0))
    cls = lambda d: pl.BlockSpec((None, d, tm // d, AW), lambda b, i: (b, 0, i, 0))
    vec = lambda n: pl.BlockSpec((1, n), lambda b, i: (0, 0))
    tab = pl.BlockSpec((tm, HEAD_DIM), lambda b, i: (i, 0))
    nat = jax.ShapeDtypeStruct((B, S, AW), BF16)
    c4 = jax.ShapeDtypeStruct((B, 4, S // 4, AW), BF16)
    c16 = jax.ShapeDtypeStruct((B, 16, S // 16, AW), BF16)
    return pl.pallas_call(
        functools.partial(_inproj_kernel, tm=tm, widths=widths, col_chunk=512),
        grid=(B, S // tm),
        in_specs=[row(D), vec(D),
                  pl.BlockSpec((None,) + w.shape[1:], lambda b, i: (l, 0, 0), pipeline_mode=pl.Buffered(1)),
                  vec(HEAD_DIM), vec(HEAD_DIM), vec(HEAD_DIM), tab, tab],
        out_specs=[row(FW), row(AW), row(AW), row(AW), cls(4), cls(4), cls(4), cls(16), cls(16), cls(16),
                   row(MW)],
        out_shape=[jax.ShapeDtypeStruct((B, S, FW), F32), nat, nat, nat, c4, c4, c4, c16, c16, c16,
                   jax.ShapeDtypeStruct((B, S, MW), BF16)],
        scratch_shapes=[pltpu.VMEM((INPROJ_SLABS, tm, HEAD_DIM), F32),
                        pltpu.VMEM((INPROJ_SLABS, tm, HEAD_DIM), F32)],
        compiler_params=_params("parallel", "parallel"),
        name="inproj",
    )(x, g.reshape(1, D), w, gq.reshape(1, -1), gk.reshape(1, -1), gmq.reshape(1, -1), cos2, sin2)


def _fft1_kernel(k_ref, x_ref, o_ref):
    n1, rows, c = x_ref.shape
    r = FFT_ROWS
    for j in range(0, rows, r):
        x = x_ref[:, j:j + r, :].reshape(n1 * r, c).astype(BF16)
        o_ref[:, :, j:j + r, :] = _dot(k_ref[...], x).reshape(2, n1, r, c)


def _fft1(f_in, k1r):
    B, S, C = f_in.shape
    N2 = FFT_N2
    N1 = S // N2
    R = FFT1_ROWS_PER_STEP
    return pl.pallas_call(
        _fft1_kernel,
        grid=(B, N2 // R),
        in_specs=[pl.BlockSpec(k1r.shape, lambda b, j: (0, 0)),
                  pl.BlockSpec((None, N1, R, C), lambda b, j: (b, 0, j, 0))],
        out_specs=pl.BlockSpec((None, 2, N1, R, C), lambda b, j: (b, 0, 0, j, 0)),
        out_shape=jax.ShapeDtypeStruct((B, 2, N1, N2, C), F32),
        compiler_params=_params("parallel", "parallel"),
        name="fft1",
    )(k1r, f_in.reshape(B, N1, N2, C))


def _fft2_kernel(y_ref, tc_ref, ts_ref, m2_ref, p_ref, cbd_ref, sbd_ref, wf_ref, go_ref, o_ref, *, G, N2, scale):
    c = y_ref.shape[-1]
    reps = c // HEAD_DIM
    br, bi = y_ref[0], y_ref[1]
    tc = jnp.concatenate([tc_ref[...]] * reps, axis=1)
    ts = jnp.concatenate([ts_ref[...]] * reps, axis=1)
    pr = (br * tc + bi * ts).astype(BF16)
    pi = (bi * tc - br * ts).astype(BF16)
    blocks = lambda t: [t[g * N2:(g + 1) * N2] for g in range(G)]
    stacked = jnp.concatenate([jnp.concatenate(blocks(pr), axis=1), jnp.concatenate(blocks(pi), axis=1)], axis=0)
    z = _dot(m2_ref[...], stacked)
    rows = lambda t: jnp.concatenate([t[:, g * c:(g + 1) * c] for g in range(G)], axis=0)
    zr, zi = rows(z[:N2]).astype(BF16), rows(z[N2:]).astype(BF16)
    y = _dot(zr, cbd_ref[...]) + _dot(zi, sbd_ref[...])
    yf = _dot((y * scale).astype(BF16), wf_ref[...])
    o = _rms(yf, go_ref[...]).astype(BF16)
    o_ref[...] = _dot(p_ref[...], o).reshape(N2, G, c)


def _fft2(y1, tc, ts, m2, cbd, sbd, wf, l, go):
    B, _, N1, N2, C = y1.shape
    G = FFT_ROWS
    S = N1 * N2
    rows = G * N2
    full = lambda a: pl.BlockSpec(a.shape, lambda b, kb: (0,) * a.ndim)
    tw = pl.BlockSpec((rows, HEAD_DIM), lambda b, kb: (kb, 0))
    t = jnp.arange(rows, dtype=jnp.int32)
    perm = jax.nn.one_hot((t % G) * N2 + t // G, rows, dtype=BF16)
    out = pl.pallas_call(
        functools.partial(_fft2_kernel, G=G, N2=N2, scale=1.0 / math.sqrt(S * HEAD_DIM)),
        grid=(B, N1 // G),
        in_specs=[pl.BlockSpec((None, 2, rows, C), lambda b, kb: (b, 0, kb, 0)), tw, tw,
                  full(m2), full(perm), full(cbd), full(sbd),
                  pl.BlockSpec((None, C, C), lambda b, kb: (l, 0, 0)),
                  pl.BlockSpec((1, C), lambda b, kb: (0, 0))],
        out_specs=pl.BlockSpec((None, N2, G, C), lambda b, kb: (b, 0, kb, 0)),
        out_shape=jax.ShapeDtypeStruct((B, N2, N1, C), F32),
        compiler_params=_params("parallel", "parallel"),
        name="fft2",
    )(y1.reshape(B, 2, S, C), tc, ts, m2, perm, cbd, sbd, wf, go.reshape(1, C))
    return out.reshape(B, S, C)


def _dft_tables(S, C):
    N2 = FFT_N2
    N1 = S // N2
    R = FFT_ROWS

    def cs(n):
        j = jnp.arange(n, dtype=jnp.int32)
        ang = ((j[:, None] * j[None, :]) % n).astype(F32) * (2.0 * math.pi / n)
        return jnp.cos(ang), jnp.sin(ang)

    c1, s1 = cs(N1)
    f1 = jnp.concatenate([c1, -s1], axis=0)
    ri = jnp.arange(2 * N1 * R, dtype=jnp.int32)[:, None]
    ci = jnp.arange(N1 * R, dtype=jnp.int32)[None, :]
    pick_row = (ri // R == jnp.arange(2 * N1, dtype=jnp.int32)[None, :]).astype(F32)
    pick_col = (jnp.arange(N1, dtype=jnp.int32)[:, None] == ci // R).astype(F32)
    spread = jnp.dot(jnp.dot(pick_row, f1, precision=lax.Precision.HIGHEST), pick_col,
                     precision=lax.Precision.HIGHEST)
    k1r = jnp.where(ri % R == ci % R, spread, 0.0).astype(BF16)
    c2, s2 = cs(N2)
    m2 = jnp.concatenate([jnp.concatenate([c2, s2], axis=1),
                          jnp.concatenate([-s2, c2], axis=1)], axis=0).astype(BF16)
    cc, sc = cs(HEAD_DIM)
    eye_g = jnp.eye(C // HEAD_DIM, dtype=F32)
    cbd = jnp.kron(eye_g, cc).astype(BF16)
    sbd = jnp.kron(eye_g, sc).astype(BF16)
    k1 = jnp.arange(N1, dtype=jnp.int32)[:, None]
    n2 = jnp.arange(N2, dtype=jnp.int32)[None, :]
    ang = (k1 * n2).astype(F32).reshape(S, 1) * (2.0 * math.pi / S)
    tc = jnp.broadcast_to(jnp.cos(ang), (S, HEAD_DIM))
    ts = jnp.broadcast_to(jnp.sin(ang), (S, HEAD_DIM))
    return k1r, m2, cbd, sbd, tc, ts


def _band_kernel(*refs, L, heads, qb, seqs, halo):
    if halo:
        q_ref, k_ref, v_ref, kp_ref, vp_ref, kn_ref, vn_ref, o_ref, st_ref = refs
    else:
        q_ref, k_ref, v_ref, o_ref, st_ref = refs
    step = pl.program_id(1)
    ones = jnp.ones((K_WINDOW, HEAD_DIM), BF16)
    lane = lax.broadcasted_iota(jnp.int32, (Q_BLOCK, HEAD_DIM), 1)
    col = lax.broadcasted_iota(jnp.int32, (Q_BLOCK, K_WINDOW), 1)
    row_minus_col = lax.broadcasted_iota(jnp.int32, (Q_BLOCK, K_WINDOW), 0) - col

    def window(main_ref, prev_ref, next_ref, b, sl):
        lo, hi = b * Q_BLOCK - HALF_WINDOW, b * Q_BLOCK + K_WINDOW - HALF_WINDOW
        parts = []
        if lo < 0:
            parts.append(prev_ref[:, sl])
        parts.append(main_ref[max(lo, 0):min(hi, qb * Q_BLOCK), sl])
        if hi > qb * Q_BLOCK:
            parts.append(next_ref[:, sl])
        return parts[0] if len(parts) == 1 else jnp.concatenate(parts, axis=0)

    for s in range(seqs):
        for b in range(qb):
            rows = slice(b * Q_BLOCK, (b + 1) * Q_BLOCK)
            if halo:
                key = (step * qb + b) * Q_BLOCK - HALF_WINDOW + col
                ok = (jnp.abs(row_minus_col + HALF_WINDOW) <= HALF_WINDOW) & (key >= 0) & (key < L)
                q_s, o_s, st_s = q_ref, o_ref, st_ref
            else:
                ws = min(max(b * Q_BLOCK - HALF_WINDOW, 0), L - K_WINDOW)
                ok = jnp.abs(row_minus_col + (b * Q_BLOCK - ws)) <= HALF_WINDOW
                q_s, o_s, st_s = q_ref.at[s], o_ref.at[s], st_ref.at[s]
            bias = jnp.where(ok, 0.0, -jnp.inf)
            stats = jnp.where(lane < STAT_SPLIT, 0.0, 1.0)
            for h in range(heads):
                sl = slice(h * HEAD_DIM, (h + 1) * HEAD_DIM)
                if halo:
                    k = window(k_ref, kp_ref, kn_ref, b, sl)
                    v = window(v_ref, vp_ref, vn_ref, b, sl)
                else:
                    k = k_ref[s, ws:ws + K_WINDOW, sl]
                    v = v_ref[s, ws:ws + K_WINDOW, sl]
                sc = lax.dot_general(q_s[rows, sl], k, (((1,), (1,)), ((), ())),
                                     preferred_element_type=F32) + bias
                m = jnp.max(sc, axis=-1, keepdims=True)
                p = jnp.exp2(sc - m).astype(BF16)
                oa = _dot(p, jnp.concatenate([v, ones], axis=1))
                o_s[rows, sl] = oa[:, :HEAD_DIM].astype(BF16)
                stats = jnp.where(lane == h, m, jnp.where(lane == STAT_SPLIT + h, oa[:, HEAD_DIM:], stats))
            st_s[rows, :] = stats


def _band_attention(q, k, v):
    G, L, AW = q.shape
    heads = AW // HEAD_DIM
    nblk = L // Q_BLOCK
    assert L % Q_BLOCK == 0 and L >= K_WINDOW and heads <= STAT_SPLIT
    qb = max(1, min(HEAD_BLOCKS_PER_STEP // heads, nblk))
    assert nblk % qb == 0
    out_shape = [jax.ShapeDtypeStruct((G, L, AW), BF16), jax.ShapeDtypeStruct((G, L, HEAD_DIM), F32)]
    if qb == nblk:
        seqs = max(1, min(HEAD_BLOCKS_PER_STEP // (heads * qb), G))
        assert G % seqs == 0
        blk = lambda w: pl.BlockSpec((seqs, L, w), lambda g, n: (g, 0, 0))
        return pl.pallas_call(
            functools.partial(_band_kernel, L=L, heads=heads, qb=qb, seqs=seqs, halo=False),
            grid=(G // seqs, 1),
            in_specs=[blk(AW)] * 3,
            out_specs=[blk(AW), blk(HEAD_DIM)],
            out_shape=out_shape,
            compiler_params=_params("parallel", "parallel"),
            name="band_attention",
        )(q, k, v)
    main = pl.BlockSpec((None, qb * Q_BLOCK, AW), lambda g, n: (g, n, 0))
    halo_per_step = qb * Q_BLOCK // HALF_WINDOW
    prev = pl.BlockSpec((None, HALF_WINDOW, AW), lambda g, n: (g, jnp.maximum(n * halo_per_step - 1, 0), 0))
    nxt = pl.BlockSpec((None, HALF_WINDOW, AW),
                       lambda g, n: (g, jnp.minimum((n + 1) * halo_per_step, L // HALF_WINDOW - 1), 0))
    return pl.pallas_call(
        functools.partial(_band_kernel, L=L, heads=heads, qb=qb, seqs=1, halo=True),
        grid=(G, nblk // qb),
        in_specs=[main, main, main, prev, prev, nxt, nxt],
        out_specs=[main, pl.BlockSpec((None, qb * Q_BLOCK, HEAD_DIM), lambda g, n: (g, n, 0))],
        out_shape=out_shape,
        compiler_params=_params("parallel", "parallel"),
        name="band_attention",
    )(q, k, v, k, v, k, v)


def _memkv_kernel(m_ref, g_ref, w_ref, gk_ref, k_ref, v_ref, *, MW):
    h = _rms(m_ref[...], g_ref[...]).astype(BF16)
    kv = _dot(h, w_ref[...])
    for j in range(MW // HEAD_DIM):
        sl = slice(j * HEAD_DIM, (j + 1) * HEAD_DIM)
        k_ref[:, sl] = _rms(kv[:, sl], gk_ref[...]).astype(BF16)
    v_ref[...] = kv[:, MW:].astype(BF16)


def _memkv(mem, g, w, l, gk, *, tm=256):
    B, M, D = mem.shape
    MW = w.shape[2] // 2
    rows = B * M
    tm = min(tm, rows)
    assert rows % tm == 0
    out = pl.BlockSpec((tm, MW), lambda i: (i, 0))
    return pl.pallas_call(
        functools.partial(_memkv_kernel, MW=MW),
        grid=(rows // tm,),
        in_specs=[pl.BlockSpec((tm, D), lambda i: (i, 0)),
                  pl.BlockSpec((1, D), lambda i: (0, 0)),
                  pl.BlockSpec((None,) + w.shape[1:], lambda i: (l, 0, 0)),
                  pl.BlockSpec((1, HEAD_DIM), lambda i: (0, 0))],
        out_specs=[out, out],
        out_shape=[jax.ShapeDtypeStruct((rows, MW), BF16)] * 2,
        compiler_params=_params("parallel"),
        name="memkv",
    )(mem.reshape(rows, D), g.reshape(1, D), w, gk.reshape(1, -1))


def _memory_attention(q_ref, k_ref, v_ref, go_ref):
    ones = jnp.ones((k_ref.shape[0], HEAD_DIM), BF16)
    parts = []
    for j in range(q_ref.shape[1] // HEAD_DIM):
        sl = slice(j * HEAD_DIM, (j + 1) * HEAD_DIM)
        s = lax.dot_general(q_ref[:, sl], k_ref[:, sl], (((1,), (1,)), ((), ())), preferred_element_type=F32)
        p = jnp.exp2(s - jnp.max(s, axis=-1, keepdims=True)).astype(BF16)
        oa = _dot(p, jnp.concatenate([v_ref[:, sl], ones], axis=1))
        parts.append(oa[:, :HEAD_DIM] * (1.0 / oa[:, HEAD_DIM:]))
    return _rms(jnp.concatenate(parts, axis=1), go_ref[...]).astype(BF16)


def _outproj_kernel(x_ref, yf_ref, o1_ref, o4_ref, o16_ref, l1_ref, l4_ref, l16_ref, mq_ref, mk_ref, mv_ref,
                    go_ref, gom_ref, p4_ref, p16_ref, e_ref, w_ref, out_ref, s4_ref, s16_ref, *, tm, AW):
    FW = yf_ref.shape[-1]
    ym = _memory_attention(mq_ref, mk_ref, mv_ref, gom_ref)
    for dil, src, dst in ((4, l4_ref, s4_ref), (16, l16_ref, s16_ref)):
        for r in range(dil):
            dst[pl.ds(r, tm // dil, stride=dil), :] = src[r]
    o1 = o1_ref[...]
    o4 = _dot(p4_ref[...], o4_ref[...].reshape(tm, AW))
    o16 = _dot(p16_ref[...], o16_ref[...].reshape(tm, AW))

    low = lax.broadcasted_iota(jnp.int32, (tm, HEAD_DIM), 1) < STAT_SPLIT

    def split(st):
        return jnp.where(low, st, 0.0), jnp.where(low, pltpu.roll(st, STAT_SPLIT, 1), 1.0)

    (m1, d1), (m4, d4), (m16, d16) = split(l1_ref[...]), split(s4_ref[...]), split(s16_ref[...])
    top = jnp.maximum(jnp.maximum(m1, m4), m16)
    e1, e4, e16 = jnp.exp2(m1 - top), jnp.exp2(m4 - top), jnp.exp2(m16 - top)
    inv = 1.0 / (e1 * d1 + e4 * d4 + e16 * d16)

    def spread(w):
        hi = w.astype(BF16)
        lo = (w - hi.astype(F32)).astype(BF16)
        return _dot(jnp.concatenate([hi, lo], axis=1), e_ref[...])

    y = spread(e1 * inv) * o1 + spread(e4 * inv) * o4 + spread(e16 * inv) * o16
    ya = _rms(y, go_ref[...]).astype(BF16)
    out_ref[...] = (x_ref[...] + _dot(yf_ref[...].astype(BF16), w_ref[:FW]) + _dot(ya, w_ref[FW:FW + AW])
                    + _dot(ym, w_ref[FW + AW:]))


def _unpermute_matrix(tm, dil):
    t = jnp.arange(tm, dtype=jnp.int32)
    return jax.nn.one_hot((t % dil) * (tm // dil) + t // dil, tm, dtype=BF16)


def _outproj(x, yf, o, lse, mq, mk, mv, go_a, go_m, w, l, *, tm=256):
    B, S, D = x.shape
    FW, AW, MW = yf.shape[-1], o[0].shape[-1], mq.shape[-1]
    M = mk.shape[1]
    assert S % tm == 0 and tm % 256 == 0 and all(t.shape[-1] == HEAD_DIM for t in lse)
    row = lambda w_: pl.BlockSpec((None, tm, w_), lambda b, i: (b, i, 0))
    cls = lambda d, w_: pl.BlockSpec((None, d, tm // d, w_), lambda b, i: (b, 0, i, 0))
    full = lambda a: pl.BlockSpec(a.shape, lambda b, i: (0,) * a.ndim)
    mem = pl.BlockSpec((None, M, MW), lambda b, i: (b, 0, 0))
    p4, p16 = _unpermute_matrix(tm, 4), _unpermute_matrix(tm, 16)
    spread = (jnp.arange(2 * HEAD_DIM, dtype=jnp.int32)[:, None] % HEAD_DIM
              == jnp.arange(AW, dtype=jnp.int32)[None, :] // HEAD_DIM).astype(BF16)
    return pl.pallas_call(
        functools.partial(_outproj_kernel, tm=tm, AW=AW),
        grid=(B, S // tm),
        in_specs=[row(D), row(FW), row(AW), cls(4, AW), cls(16, AW), row(HEAD_DIM), cls(4, HEAD_DIM),
                  cls(16, HEAD_DIM), row(MW), mem, mem, pl.BlockSpec((1, AW), lambda b, i: (0, 0)),
                  pl.BlockSpec((1, MW), lambda b, i: (0, 0)), full(p4), full(p16), full(spread),
                  pl.BlockSpec((None,) + w.shape[1:], lambda b, i: (l, 0, 0))],
        out_specs=row(D),
        out_shape=jax.ShapeDtypeStruct((B, S, D), F32),
        scratch_shapes=[pltpu.VMEM((tm, HEAD_DIM), F32)] * 2,
        compiler_params=_params("parallel", "parallel"),
        name="outproj",
    )(x, yf, *o, *lse, mq, mk, mv, go_a.reshape(1, AW), go_m.reshape(1, MW), p4, p16, spread, w)


def _rope_tables(S):
    inv = ROPE_THETA ** (-jnp.arange(0, HEAD_DIM, 2, dtype=F32) / HEAD_DIM)
    ang = jnp.arange(S, dtype=F32)[:, None] * inv[None, :]
    cos, sin = jnp.cos(ang), jnp.sin(ang)
    return jnp.concatenate([cos, cos], axis=1), jnp.concatenate([-sin, sin], axis=1)


def _trunk(x, mem, w, g, widths):
    B, S, D = x.shape
    FW, AW, MW = widths
    M = mem.shape[1]
    cos2, sin2 = _rope_tables(S)
    k1r, m2, cbd, sbd, tc, ts = _dft_tables(S, FW)
    for l in range(w["in"].shape[0]):
        x = _ffn(x.reshape(B * S, D), g["ffn1"][l], *w["ffn1"], l).reshape(B, S, D)

        f_in, *qkv, mq = _inproj(x, g["mix"][l], w["in"], l, widths, g["q"][l], g["k"][l], g["mq"][l],
                                 cos2, sin2)

        g_out = g["out"][l]
        y_f = _fft2(_fft1(f_in, k1r), tc, ts, m2, cbd, sbd, w["fourier"], l, g_out[:FW])

        outs, lses = [], []
        for i, dil in enumerate(DILATIONS):
            q, k, v = (t.reshape(B * dil, S // dil, AW) for t in qkv[3 * i:3 * i + 3])
            o, lse = _band_attention(q, k, v)
            lead = (B, S) if dil == 1 else (B, dil, S // dil)
            outs.append(o.reshape(*lead, AW))
            lses.append(lse.reshape(*lead, HEAD_DIM))

        mk, mv = _memkv(mem, g["mem"][l], w["mem_kv"], l, g["mk"][l])
        x = _outproj(x, y_f, outs, lses, mq, mk.reshape(B, M, MW), mv.reshape(B, M, MW),
                     g_out[FW:FW + AW], g_out[FW + AW:], w["out"], l)
        x = _ffn(x.reshape(B * S, D), g["ffn2"][l], *w["ffn2"], l).reshape(B, S, D)
    return x


def kernel(x_prompt, x_sample, mem_prompt, mem_sample, g_ffn1, w_ffn1_gate, w_ffn1_up, w_ffn1_down, g_mix, w_in, g_q, g_k, g_mem, w_mem_kv, g_mq, g_mk, w_fourier, g_out, w_out, g_ffn2, w_ffn2_gate, w_ffn2_up, w_ffn2_down):
    FW = w_fourier.shape[2]
    MW = w_mem_kv.shape[2] // 2
    AW = (w_in.shape[2] - FW - MW) // 3
    b16 = lambda a: a.astype(BF16)
    w = {"ffn1": (b16(w_ffn1_gate), b16(w_ffn1_up), b16(w_ffn1_down)),
         "ffn2": (b16(w_ffn2_gate), b16(w_ffn2_up), b16(w_ffn2_down)),
         "in": b16(w_in), "out": b16(w_out), "mem_kv": b16(w_mem_kv), "fourier": b16(w_fourier)}
    g = {"ffn1": g_ffn1, "ffn2": g_ffn2, "mix": g_mix, "q": g_q, "k": g_k, "mq": g_mq, "mk": g_mk,
         "mem": g_mem, "out": g_out}
    widths = (FW, AW, MW)
    return (_trunk(x_prompt, mem_prompt, w, g, widths), _trunk(x_sample, mem_sample, w, g, widths))
```

```python
import functools
import math

import jax
import jax.numpy as jnp
from jax import lax
from jax.experimental import pallas as pl
from jax.experimental.pallas import tpu as pltpu

EPS = 1e-6
HEAD_DIM = 128
ROPE_THETA = 10000.0
DILATIONS = (1, 4, 16)
HALF_WINDOW = 64
Q_BLOCK = 2 * HALF_WINDOW
K_WINDOW = 2 * Q_BLOCK
FFT_N2 = 64
FFT_ROWS = 8
Q_SCALE = HEAD_DIM ** -0.5 * math.log2(math.e)
FFT1_ROWS_PER_STEP = 16
HEAD_BLOCKS_PER_STEP = 128
STAT_SPLIT = HEAD_DIM // 2
INPROJ_SLABS = 8

BF16 = jnp.bfloat16
F32 = jnp.float32

VMEM_LIMIT_BYTES = 56 * 1024 * 1024


def _params(*semantics):
    return pltpu.CompilerParams(dimension_semantics=semantics, vmem_limit_bytes=VMEM_LIMIT_BYTES)


def _rms(x, g):
    return x * lax.rsqrt(jnp.mean(x * x, axis=-1, keepdims=True) + EPS) * g


def _dot(a, b):
    return jnp.dot(a, b, preferred_element_type=F32)


def _ffn_kernel(x_ref, g_ref, wg_ref, wu_ref, wd_ref, o_ref, h_ref):
    f = pl.program_id(1)

    @pl.when(f == 0)
    def _():
        x = x_ref[...]
        h_ref[...] = _rms(x, g_ref[...]).astype(BF16)
        o_ref[...] = x

    h = h_ref[...]
    gate = _dot(h, wg_ref[...])
    up = _dot(h, wu_ref[...])
    a = (0.5 * gate * (1.0 / (1.0 + jnp.exp(-gate))) * up).astype(BF16)
    o_ref[...] += _dot(a, wd_ref[...])


def _ffn(x, g, wg, wu, wd, l, *, tm=1024, tf=512):
    T, D = x.shape
    F = wg.shape[2]
    tf = min(tf, F)
    assert T % tm == 0 and F % tf == 0
    return pl.pallas_call(
        _ffn_kernel,
        grid=(T // tm, F // tf),
        in_specs=[
            pl.BlockSpec((tm, D), lambda i, f: (i, 0)),
            pl.BlockSpec((1, D), lambda i, f: (0, 0)),
            pl.BlockSpec((None, D, tf), lambda i, f: (l, 0, f)),
            pl.BlockSpec((None, D, tf), lambda i, f: (l, 0, f)),
            pl.BlockSpec((None, tf, D), lambda i, f: (l, f, 0)),
        ],
        out_specs=pl.BlockSpec((tm, D), lambda i, f: (i, 0)),
        out_shape=jax.ShapeDtypeStruct((T, D), F32),
        scratch_shapes=[pltpu.VMEM((tm, D), BF16)],
        compiler_params=_params("parallel", "arbitrary"),
        name="ffn",
    )(x, g.reshape(1, D), wg, wu, wd)


def _inproj_kernel(x_ref, g_ref, w_ref, gq_ref, gk_ref, gmq_ref, cos_ref, sin_ref,
                   f_ref, qkv1_ref, qkv4_ref, qkv16_ref, mq_ref, nat_ref, cm4_ref, *, tm, widths, col_chunk):
    FW, AW, MW = widths
    h = _rms(x_ref[...], g_ref[...]).astype(BF16)
    cos = cos_ref[...]
    sin = sin_ref[...]
    slabs = nat_ref.shape[0]

    def heads(col0, width, emit, fn):
        cc = min(col_chunk, width)
        for c0 in range(0, width, cc):
            p = _dot(h, w_ref[:, col0 + c0:col0 + c0 + cc])
            for j in range(cc // HEAD_DIM):
                emit(c0 // HEAD_DIM + j, fn(p[:, j * HEAD_DIM:(j + 1) * HEAD_DIM]))

    def emit_f(head, t):
        f_ref[:, head * HEAD_DIM:(head + 1) * HEAD_DIM] = t

    def class_copies(base):
        def emit(head, t):
            sl = slice((base + head) * HEAD_DIM, (base + head + 1) * HEAD_DIM)
            slab = (base + head) % slabs
            qkv1_ref[:, sl] = t.astype(BF16)
            nat_ref[slab] = t
            for r4 in range(4):
                c4 = nat_ref[slab, pl.ds(r4, tm // 4, stride=4), :]
                qkv4_ref[r4, :, sl] = c4.astype(BF16)
                cm4_ref[slab, r4 * (tm // 4):(r4 + 1) * (tm // 4), :] = c4
            for r16 in range(16):
                start = (r16 % 4) * (tm // 4) + r16 // 4
                c16 = cm4_ref[slab, pl.ds(start, tm // 16, stride=4), :]
                qkv16_ref[r16, :, sl] = c16.astype(BF16)
        return emit

    def norm_rope(g_ref_, post_scale):
        def fn(t):
            t = _rms(t, g_ref_[...])
            t = t * cos + pltpu.roll(t, HEAD_DIM // 2, 1) * sin
            return t if post_scale is None else t * post_scale
        return fn

    def emit_mq(head, t):
        mq_ref[:, head * HEAD_DIM:(head + 1) * HEAD_DIM] = t.astype(BF16)

    nh = AW // HEAD_DIM
    heads(0, FW, emit_f, lambda t: t)
    heads(FW, AW, class_copies(0), norm_rope(gq_ref, Q_SCALE))
    heads(FW + AW, AW, class_copies(nh), norm_rope(gk_ref, None))
    heads(FW + 3 * AW, MW, emit_mq, lambda t: _rms(t, gmq_ref[...]) * Q_SCALE)
    heads(FW + 2 * AW, AW, class_copies(2 * nh), lambda t: t)


def _inproj(x, g, w, l, widths, gq, gk, gmq, cos2, sin2, *, tm=512):
    B, S, D = x.shape
    FW, AW, MW = widths
    assert S % tm == 0 and tm % 256 == 0
    row = lambda w_: pl.BlockSpec((None, tm, w_), lambda b, i: (b, i, 0))
    cls = lambda d: pl.BlockSpec((None, d, tm // d, 3 * AW), lambda b, i: (b, 0, i, 0))
    vec = lambda n: pl.BlockSpec((1, n), lambda b, i: (0, 0))
    tab = pl.BlockSpec((tm, HEAD_DIM), lambda b, i: (i, 0))
    nat = jax.ShapeDtypeStruct((B, S, 3 * AW), BF16)
    c4 = jax.ShapeDtypeStruct((B, 4, S // 4, 3 * AW), BF16)
    c16 = jax.ShapeDtypeStruct((B, 16, S // 16, 3 * AW), BF16)
    return pl.pallas_call(
        functools.partial(_inproj_kernel, tm=tm, widths=widths, col_chunk=512),
        grid=(B, S // tm),
        in_specs=[row(D), vec(D),
                  pl.BlockSpec((None,) + w.shape[1:], lambda b, i: (l, 0, 0), pipeline_mode=pl.Buffered(1)),
                  vec(HEAD_DIM), vec(HEAD_DIM), vec(HEAD_DIM), tab, tab],
        out_specs=[row(FW), row(3 * AW), cls(4), cls(16), row(MW)],
        out_shape=[jax.ShapeDtypeStruct((B, S, FW), F32), nat, c4, c16, jax.ShapeDtypeStruct((B, S, MW), BF16)],
        scratch_shapes=[pltpu.VMEM((INPROJ_SLABS, tm, HEAD_DIM), F32),
                        pltpu.VMEM((INPROJ_SLABS, tm, HEAD_DIM), F32)],
        compiler_params=_params("parallel", "parallel"),
        name="inproj",
    )(x, g.reshape(1, D), w, gq.reshape(1, -1), gk.reshape(1, -1), gmq.reshape(1, -1), cos2, sin2)


def _fft1_kernel(k_ref, x_ref, o_ref):
    n1, rows, c = x_ref.shape
    r = FFT_ROWS
    for j in range(0, rows, r):
        x = x_ref[:, j:j + r, :].reshape(n1 * r, c).astype(BF16)
        o_ref[:, :, j:j + r, :] = _dot(k_ref[...], x).reshape(2, n1, r, c)


def _fft1(f_in, k1r):
    B, S, C = f_in.shape
    N2 = FFT_N2
    N1 = S // N2
    R = FFT1_ROWS_PER_STEP
    return pl.pallas_call(
        _fft1_kernel,
        grid=(B, N2 // R),
        in_specs=[pl.BlockSpec(k1r.shape, lambda b, j: (0, 0)),
                  pl.BlockSpec((None, N1, R, C), lambda b, j: (b, 0, j, 0))],
        out_specs=pl.BlockSpec((None, 2, N1, R, C), lambda b, j: (b, 0, 0, j, 0)),
        out_shape=jax.ShapeDtypeStruct((B, 2, N1, N2, C), F32),
        compiler_params=_params("parallel", "parallel"),
        name="fft1",
    )(k1r, f_in.reshape(B, N1, N2, C))


def _fft2_kernel(y_ref, tc_ref, ts_ref, m2_ref, p_ref, cbd_ref, sbd_ref, wf_ref, go_ref, o_ref, *, G, N2, scale):
    c = y_ref.shape[-1]
    reps = c // HEAD_DIM
    br, bi = y_ref[0], y_ref[1]
    tc = jnp.concatenate([tc_ref[...]] * reps, axis=1)
    ts = jnp.concatenate([ts_ref[...]] * reps, axis=1)
    pr = (br * tc + bi * ts).astype(BF16)
    pi = (bi * tc - br * ts).astype(BF16)
    blocks = lambda t: [t[g * N2:(g + 1) * N2] for g in range(G)]
    stacked = jnp.concatenate([jnp.concatenate(blocks(pr), axis=1), jnp.concatenate(blocks(pi), axis=1)], axis=0)
    z = _dot(m2_ref[...], stacked)
    rows = lambda t: jnp.concatenate([t[:, g * c:(g + 1) * c] for g in range(G)], axis=0)
    zr, zi = rows(z[:N2]).astype(BF16), rows(z[N2:]).astype(BF16)
    y = _dot(zr, cbd_ref[...]) + _dot(zi, sbd_ref[...])
    yf = _dot((y * scale).astype(BF16), wf_ref[...])
    o = _rms(yf, go_ref[...]).astype(BF16)
    o_ref[...] = _dot(p_ref[...], o).reshape(N2, G, c)


def _fft2(y1, tc, ts, m2, cbd, sbd, wf, l, go):
    B, _, N1, N2, C = y1.shape
    G = FFT_ROWS
    S = N1 * N2
    rows = G * N2
    full = lambda a: pl.BlockSpec(a.shape, lambda b, kb: (0,) * a.ndim)
    tw = pl.BlockSpec((rows, HEAD_DIM), lambda b, kb: (kb, 0))
    t = jnp.arange(rows, dtype=jnp.int32)
    perm = jax.nn.one_hot((t % G) * N2 + t // G, rows, dtype=BF16)
    out = pl.pallas_call(
        functools.partial(_fft2_kernel, G=G, N2=N2, scale=1.0 / math.sqrt(S * HEAD_DIM)),
        grid=(B, N1 // G),
        in_specs=[pl.BlockSpec((None, 2, rows, C), lambda b, kb: (b, 0, kb, 0)), tw, tw,
                  full(m2), full(perm), full(cbd), full(sbd),
                  pl.BlockSpec((None, C, C), lambda b, kb: (l, 0, 0)),
                  pl.BlockSpec((1, C), lambda b, kb: (0, 0))],
        out_specs=pl.BlockSpec((None, N2, G, C), lambda b, kb: (b, 0, kb, 0)),
        out_shape=jax.ShapeDtypeStruct((B, N2, N1, C), F32),
        compiler_params=_params("parallel", "parallel"),
        name="fft2",
    )(y1.reshape(B, 2, S, C), tc, ts, m2, perm, cbd, sbd, wf, go.reshape(1, C))
    return out.reshape(B, S, C)


def _dft_tables(S, C):
    N2 = FFT_N2
    N1 = S // N2
    R = FFT_ROWS

    def cs(n):
        j = jnp.arange(n, dtype=jnp.int32)
        ang = ((j[:, None] * j[None, :]) % n).astype(F32) * (2.0 * math.pi / n)
        return jnp.cos(ang), jnp.sin(ang)

    c1, s1 = cs(N1)
    f1 = jnp.concatenate([c1, -s1], axis=0)
    ri = jnp.arange(2 * N1 * R, dtype=jnp.int32)[:, None]
    ci = jnp.arange(N1 * R, dtype=jnp.int32)[None, :]
    pick_row = (ri // R == jnp.arange(2 * N1, dtype=jnp.int32)[None, :]).astype(F32)
    pick_col = (jnp.arange(N1, dtype=jnp.int32)[:, None] == ci // R).astype(F32)
    spread = jnp.dot(jnp.dot(pick_row, f1, precision=lax.Precision.HIGHEST), pick_col,
                     precision=lax.Precision.HIGHEST)
    k1r = jnp.where(ri % R == ci % R, spread, 0.0).astype(BF16)
    c2, s2 = cs(N2)
    m2 = jnp.concatenate([jnp.concatenate([c2, s2], axis=1),
                          jnp.concatenate([-s2, c2], axis=1)], axis=0).astype(BF16)
    cc, sc = cs(HEAD_DIM)
    eye_g = jnp.eye(C // HEAD_DIM, dtype=F32)
    cbd = jnp.kron(eye_g, cc).astype(BF16)
    sbd = jnp.kron(eye_g, sc).astype(BF16)
    k1 = jnp.arange(N1, dtype=jnp.int32)[:, None]
    n2 = jnp.arange(N2, dtype=jnp.int32)[None, :]
    ang = (k1 * n2).astype(F32).reshape(S, 1) * (2.0 * math.pi / S)
    tc = jnp.broadcast_to(jnp.cos(ang), (S, HEAD_DIM))
    ts = jnp.broadcast_to(jnp.sin(ang), (S, HEAD_DIM))
    return k1r, m2, cbd, sbd, tc, ts


def _band_kernel(*refs, L, heads, qb, seqs, halo):
    if halo:
        q_ref, k_ref, v_ref, kp_ref, vp_ref, kn_ref, vn_ref, o_ref, st_ref = refs
    else:
        q_ref, k_ref, v_ref, o_ref, st_ref = refs
    step = pl.program_id(1)
    ones = jnp.ones((K_WINDOW, HEAD_DIM), BF16)
    lane = lax.broadcasted_iota(jnp.int32, (Q_BLOCK, HEAD_DIM), 1)
    col = lax.broadcasted_iota(jnp.int32, (Q_BLOCK, K_WINDOW), 1)
    row_minus_col = lax.broadcasted_iota(jnp.int32, (Q_BLOCK, K_WINDOW), 0) - col

    def window(main_ref, prev_ref, next_ref, b, sl):
        lo, hi = b * Q_BLOCK - HALF_WINDOW, b * Q_BLOCK + K_WINDOW - HALF_WINDOW
        parts = []
        if lo < 0:
            parts.append(prev_ref[:, sl])
        parts.append(main_ref[max(lo, 0):min(hi, qb * Q_BLOCK), sl])
        if hi > qb * Q_BLOCK:
            parts.append(next_ref[:, sl])
        return parts[0] if len(parts) == 1 else jnp.concatenate(parts, axis=0)

    for s in range(seqs):
        for b in range(qb):
            rows = slice(b * Q_BLOCK, (b + 1) * Q_BLOCK)
            if halo:
                key = (step * qb + b) * Q_BLOCK - HALF_WINDOW + col
                ok = (jnp.abs(row_minus_col + HALF_WINDOW) <= HALF_WINDOW) & (key >= 0) & (key < L)
                q_s, o_s, st_s = q_ref, o_ref, st_ref
            else:
                ws = min(max(b * Q_BLOCK - HALF_WINDOW, 0), L - K_WINDOW)
                ok = jnp.abs(row_minus_col + (b * Q_BLOCK - ws)) <= HALF_WINDOW
                q_s, o_s, st_s = q_ref.at[s], o_ref.at[s], st_ref.at[s]
            bias = jnp.where(ok, 0.0, -jnp.inf)
            stats = jnp.where(lane < STAT_SPLIT, 0.0, 1.0)
            for h in range(heads):
                sl = slice(h * HEAD_DIM, (h + 1) * HEAD_DIM)
                if halo:
                    k = window(k_ref, kp_ref, kn_ref, b, sl)
                    v = window(v_ref, vp_ref, vn_ref, b, sl)
                else:
                    k = k_ref[s, ws:ws + K_WINDOW, sl]
                    v = v_ref[s, ws:ws + K_WINDOW, sl]
                sc = lax.dot_general(q_s[rows, sl], k, (((1,), (1,)), ((), ())),
                                     preferred_element_type=F32) + bias
                m = jnp.max(sc, axis=-1, keepdims=True)
                p = jnp.exp2(sc - m).astype(BF16)
                oa = _dot(p, jnp.concatenate([v, ones], axis=1))
                o_s[rows, sl] = oa[:, :HEAD_DIM].astype(BF16)
                stats = jnp.where(lane == h, m, jnp.where(lane == STAT_SPLIT + h, oa[:, HEAD_DIM:], stats))
            st_s[rows, :] = stats


def _band_attention(qkv):
    G, L, AW = qkv.shape[0], qkv.shape[1], qkv.shape[2] // 3
    heads = AW // HEAD_DIM
    nblk = L // Q_BLOCK
    assert L % Q_BLOCK == 0 and L >= K_WINDOW and heads <= STAT_SPLIT
    qb = max(1, min(HEAD_BLOCKS_PER_STEP // heads, nblk))
    assert nblk % qb == 0
    out_shape = [jax.ShapeDtypeStruct((G, L, AW), BF16), jax.ShapeDtypeStruct((G, L, HEAD_DIM), F32)]
    if qb == nblk:
        seqs = max(1, min(HEAD_BLOCKS_PER_STEP // (heads * qb), G))
        assert G % seqs == 0
        blk = lambda w, c: pl.BlockSpec((seqs, L, w), lambda g, n: (g, 0, c))
        return pl.pallas_call(
            functools.partial(_band_kernel, L=L, heads=heads, qb=qb, seqs=seqs, halo=False),
            grid=(G // seqs, 1),
            in_specs=[blk(AW, 0), blk(AW, 1), blk(AW, 2)],
            out_specs=[blk(AW, 0), blk(HEAD_DIM, 0)],
            out_shape=out_shape,
            compiler_params=_params("parallel", "parallel"),
            name="band_attention",
        )(qkv, qkv, qkv)
    main = lambda c: pl.BlockSpec((None, qb * Q_BLOCK, AW), lambda g, n: (g, n, c))
    halo_per_step = qb * Q_BLOCK // HALF_WINDOW
    prev = lambda c: pl.BlockSpec((None, HALF_WINDOW, AW),
                                  lambda g, n: (g, jnp.maximum(n * halo_per_step - 1, 0), c))
    nxt = lambda c: pl.BlockSpec((None, HALF_WINDOW, AW),
                                 lambda g, n: (g, jnp.minimum((n + 1) * halo_per_step, L // HALF_WINDOW - 1), c))
    return pl.pallas_call(
        functools.partial(_band_kernel, L=L, heads=heads, qb=qb, seqs=1, halo=True),
        grid=(G, nblk // qb),
        in_specs=[main(0), main(1), main(2), prev(1), prev(2), nxt(1), nxt(2)],
        out_specs=[main(0), pl.BlockSpec((None, qb * Q_BLOCK, HEAD_DIM), lambda g, n: (g, n, 0))],
        out_shape=out_shape,
        compiler_params=_params("parallel", "parallel"),
        name="band_attention",
    )(*[qkv] * 7)


def _memkv_kernel(m_ref, g_ref, w_ref, gk_ref, k_ref, v_ref, *, MW):
    h = _rms(m_ref[...], g_ref[...]).astype(BF16)
    kv = _dot(h, w_ref[...])
    for j in range(MW // HEAD_DIM):
        sl = slice(j * HEAD_DIM, (j + 1) * HEAD_DIM)
        k_ref[:, sl] = _rms(kv[:, sl], gk_ref[...]).astype(BF16)
    v_ref[...] = kv[:, MW:].astype(BF16)


def _memkv(mem, g, w, l, gk, *, tm=256):
    B, M, D = mem.shape
    MW = w.shape[2] // 2
    rows = B * M
    tm = min(tm, rows)
    assert rows % tm == 0
    out = pl.BlockSpec((tm, MW), lambda i: (i, 0))
    return pl.pallas_call(
        functools.partial(_memkv_kernel, MW=MW),
        grid=(rows // tm,),
        in_specs=[pl.BlockSpec((tm, D), lambda i: (i, 0)),
                  pl.BlockSpec((1, D), lambda i: (0, 0)),
                  pl.BlockSpec((None,) + w.shape[1:], lambda i: (l, 0, 0)),
                  pl.BlockSpec((1, HEAD_DIM), lambda i: (0, 0))],
        out_specs=[out, out],
        out_shape=[jax.ShapeDtypeStruct((rows, MW), BF16)] * 2,
        compiler_params=_params("parallel"),
        name="memkv",
    )(mem.reshape(rows, D), g.reshape(1, D), w, gk.reshape(1, -1))


def _memory_attention(q_ref, k_ref, v_ref, go_ref):
    ones = jnp.ones((k_ref.shape[0], HEAD_DIM), BF16)
    parts = []
    for j in range(q_ref.shape[1] // HEAD_DIM):
        sl = slice(j * HEAD_DIM, (j + 1) * HEAD_DIM)
        s = lax.dot_general(q_ref[:, sl], k_ref[:, sl], (((1,), (1,)), ((), ())), preferred_element_type=F32)
        p = jnp.exp2(s - jnp.max(s, axis=-1, keepdims=True)).astype(BF16)
        oa = _dot(p, jnp.concatenate([v_ref[:, sl], ones], axis=1))
        parts.append(oa[:, :HEAD_DIM] * (1.0 / oa[:, HEAD_DIM:]))
    return _rms(jnp.concatenate(parts, axis=1), go_ref[...]).astype(BF16)


def _outproj_kernel(x_ref, yf_ref, o1_ref, o4_ref, o16_ref, l1_ref, l4_ref, l16_ref, mq_ref, mk_ref, mv_ref,
                    go_ref, gom_ref, p4_ref, p16_ref, e_ref, w_ref, out_ref, s4_ref, s16_ref, *, tm, AW):
    FW = yf_ref.shape[-1]
    ym = _memory_attention(mq_ref, mk_ref, mv_ref, gom_ref)
    for dil, src, dst in ((4, l4_ref, s4_ref), (16, l16_ref, s16_ref)):
        for r in range(dil):
            dst[pl.ds(r, tm // dil, stride=dil), :] = src[r]
    o1 = o1_ref[...]
    o4 = _dot(p4_ref[...], o4_ref[...].reshape(tm, AW))
    o16 = _dot(p16_ref[...], o16_ref[...].reshape(tm, AW))

    low = lax.broadcasted_iota(jnp.int32, (tm, HEAD_DIM), 1) < STAT_SPLIT

    def split(st):
        return jnp.where(low, st, 0.0), jnp.where(low, pltpu.roll(st, STAT_SPLIT, 1), 1.0)

    (m1, d1), (m4, d4), (m16, d16) = split(l1_ref[...]), split(s4_ref[...]), split(s16_ref[...])
    top = jnp.maximum(jnp.maximum(m1, m4), m16)
    e1, e4, e16 = jnp.exp2(m1 - top), jnp.exp2(m4 - top), jnp.exp2(m16 - top)
    inv = 1.0 / (e1 * d1 + e4 * d4 + e16 * d16)

    def spread(w):
        hi = w.astype(BF16)
        lo = (w - hi.astype(F32)).astype(BF16)
        return _dot(jnp.concatenate([hi, lo], axis=1), e_ref[...])

    y = spread(e1 * inv) * o1 + spread(e4 * inv) * o4 + spread(e16 * inv) * o16
    ya = _rms(y, go_ref[...]).astype(BF16)
    out_ref[...] = (x_ref[...] + _dot(yf_ref[...].astype(BF16), w_ref[:FW]) + _dot(ya, w_ref[FW:FW + AW])
                    + _dot(ym, w_ref[FW + AW:]))


def _unpermute_matrix(tm, dil):
    t = jnp.arange(tm, dtype=jnp.int32)
    return jax.nn.one_hot((t % dil) * (tm // dil) + t // dil, tm, dtype=BF16)


def _outproj(x, yf, o, lse, mq, mk, mv, go_a, go_m, w, l, *, tm=256):
    B, S, D = x.shape
    FW, AW, MW = yf.shape[-1], o[0].shape[-1], mq.shape[-1]
    M = mk.shape[1]
    assert S % tm == 0 and tm % 256 == 0 and all(t.shape[-1] == HEAD_DIM for t in lse)
    row = lambda w_: pl.BlockSpec((None, tm, w_), lambda b, i: (b, i, 0))
    cls = lambda d, w_: pl.BlockSpec((None, d, tm // d, w_), lambda b, i: (b, 0, i, 0))
    full = lambda a: pl.BlockSpec(a.shape, lambda b, i: (0,) * a.ndim)
    mem = pl.BlockSpec((None, M, MW), lambda b, i: (b, 0, 0))
    p4, p16 = _unpermute_matrix(tm, 4), _unpermute_matrix(tm, 16)
    spread = (jnp.arange(2 * HEAD_DIM, dtype=jnp.int32)[:, None] % HEAD_DIM
              == jnp.arange(AW, dtype=jnp.int32)[None, :] // HEAD_DIM).astype(BF16)
    return pl.pallas_call(
        functools.partial(_outproj_kernel, tm=tm, AW=AW),
        grid=(B, S // tm),
        in_specs=[row(D), row(FW), row(AW), cls(4, AW), cls(16, AW), row(HEAD_DIM), cls(4, HEAD_DIM),
                  cls(16, HEAD_DIM), row(MW), mem, mem, pl.BlockSpec((1, AW), lambda b, i: (0, 0)),
                  pl.BlockSpec((1, MW), lambda b, i: (0, 0)), full(p4), full(p16), full(spread),
                  pl.BlockSpec((None,) + w.shape[1:], lambda b, i: (l, 0, 0))],
        out_specs=row(D),
        out_shape=jax.ShapeDtypeStruct((B, S, D), F32),
        scratch_shapes=[pltpu.VMEM((tm, HEAD_DIM), F32)] * 2,
        compiler_params=_params("parallel", "parallel"),
        name="outproj",
    )(x, yf, *o, *lse, mq, mk, mv, go_a.reshape(1, AW), go_m.reshape(1, MW), p4, p16, spread, w)


def _rope_tables(S):
    inv = ROPE_THETA ** (-jnp.arange(0, HEAD_DIM, 2, dtype=F32) / HEAD_DIM)
    ang = jnp.arange(S, dtype=F32)[:, None] * inv[None, :]
    cos, sin = jnp.cos(ang), jnp.sin(ang)
    return jnp.concatenate([cos, cos], axis=1), jnp.concatenate([-sin, sin], axis=1)


def _trunk(x, mem, w, g, widths):
    B, S, D = x.shape
    FW, AW, MW = widths
    M = mem.shape[1]
    cos2, sin2 = _rope_tables(S)
    k1r, m2, cbd, sbd, tc, ts = _dft_tables(S, FW)
    for l in range(w["in"].shape[0]):
        x = _ffn(x.reshape(B * S, D), g["ffn1"][l], *w["ffn1"], l).reshape(B, S, D)

        f_in, *qkv, mq = _inproj(x, g["mix"][l], w["in"], l, widths, g["q"][l], g["k"][l], g["mq"][l],
                                 cos2, sin2)

        g_out = g["out"][l]
        y_f = _fft2(_fft1(f_in, k1r), tc, ts, m2, cbd, sbd, w["fourier"], l, g_out[:FW])

        outs, lses = [], []
        for t, dil in zip(qkv, DILATIONS):
            o, lse = _band_attention(t.reshape(B * dil, S // dil, 3 * AW))
            lead = (B, S) if dil == 1 else (B, dil, S // dil)
            outs.append(o.reshape(*lead, AW))
            lses.append(lse.reshape(*lead, HEAD_DIM))

        mk, mv = _memkv(mem, g["mem"][l], w["mem_kv"], l, g["mk"][l])
        x = _outproj(x, y_f, outs, lses, mq, mk.reshape(B, M, MW), mv.reshape(B, M, MW),
                     g_out[FW:FW + AW], g_out[FW + AW:], w["out"], l)
        x = _ffn(x.reshape(B * S, D), g["ffn2"][l], *w["ffn2"], l).reshape(B, S, D)
    return x


def kernel(x_prompt, x_sample, mem_prompt, mem_sample, g_ffn1, w_ffn1_gate, w_ffn1_up, w_ffn1_down, g_mix, w_in, g_q, g_k, g_mem, w_mem_kv, g_mq, g_mk, w_fourier, g_out, w_out, g_ffn2, w_ffn2_gate, w_ffn2_up, w_ffn2_down):
    FW = w_fourier.shape[2]
    MW = w_mem_kv.shape[2] // 2
    AW = (w_in.shape[2] - FW - MW) // 3
    b16 = lambda a: a.astype(BF16)
    w = {"ffn1": (b16(w_ffn1_gate), b16(w_ffn1_up), b16(w_ffn1_down)),
         "ffn2": (b16(w_ffn2_gate), b16(w_ffn2_up), b16(w_ffn2_down)),
         "in": b16(w_in), "out": b16(w_out), "mem_kv": b16(w_mem_kv), "fourier": b16(w_fourier)}
    g = {"ffn1": g_ffn1, "ffn2": g_ffn2, "mix": g_mix, "q": g_q, "k": g_k, "mq": g_mq, "mk": g_mk,
         "mem": g_mem, "out": g_out}
    widths = (FW, AW, MW)
    return (_trunk(x_prompt, mem_prompt, w, g, widths), _trunk(x_sample, mem_sample, w, g, widths))
```

```python
import functools
import math

import jax
import jax.numpy as jnp
from jax import lax
from jax.experimental import pallas as pl
from jax.experimental.pallas import tpu as pltpu

EPS = 1e-6
HEAD_DIM = 128
ROPE_THETA = 10000.0
DILATIONS = (1, 4, 16)
HALF_WINDOW = 64
Q_BLOCK = 2 * HALF_WINDOW
K_WINDOW = 2 * Q_BLOCK
FFT_N2 = 64
FFT_ROWS = 8
Q_SCALE = HEAD_DIM ** -0.5 * math.log2(math.e)
FFT1_ROWS_PER_STEP = 16
HEAD_BLOCKS_PER_STEP = 128
STAT_SPLIT = HEAD_DIM // 2
INPROJ_SLABS = 8

BF16 = jnp.bfloat16
F32 = jnp.float32

VMEM_LIMIT_BYTES = 56 * 1024 * 1024


def _params(*semantics):
    return pltpu.CompilerParams(dimension_semantics=semantics, vmem_limit_bytes=VMEM_LIMIT_BYTES)


def _rms(x, g):
    return x * lax.rsqrt(jnp.mean(x * x, axis=-1, keepdims=True) + EPS) * g


def _dot(a, b):
    return jnp.dot(a, b, preferred_element_type=F32)


def _ffn_kernel(x_ref, g_ref, wg_ref, wu_ref, wd_ref, o_ref, h_ref):
    f = pl.program_id(1)

    @pl.when(f == 0)
    def _():
        x = x_ref[...]
        h_ref[...] = _rms(x, g_ref[...]).astype(BF16)
        o_ref[...] = x

    h = h_ref[...]
    gate = _dot(h, wg_ref[...])
    up = _dot(h, wu_ref[...])
    a = (0.5 * gate * (1.0 / (1.0 + jnp.exp(-gate))) * up).astype(BF16)
    o_ref[...] += _dot(a, wd_ref[...])


def _ffn(x, g, wg, wu, wd, l, *, tm=1024, tf=512):
    T, D = x.shape
    F = wg.shape[2]
    tf = min(tf, F)
    assert T % tm == 0 and F % tf == 0
    return pl.pallas_call(
        _ffn_kernel,
        grid=(T // tm, F // tf),
        in_specs=[
            pl.BlockSpec((tm, D), lambda i, f: (i, 0)),
            pl.BlockSpec((1, D), lambda i, f: (0, 0)),
            pl.BlockSpec((None, D, tf), lambda i, f: (l, 0, f)),
            pl.BlockSpec((None, D, tf), lambda i, f: (l, 0, f)),
            pl.BlockSpec((None, tf, D), lambda i, f: (l, f, 0)),
        ],
        out_specs=pl.BlockSpec((tm, D), lambda i, f: (i, 0)),
        out_shape=jax.ShapeDtypeStruct((T, D), F32),
        scratch_shapes=[pltpu.VMEM((tm, D), BF16)],
        compiler_params=_params("parallel", "arbitrary"),
        name="ffn",
    )(x, g.reshape(1, D), wg, wu, wd)


def _inproj_kernel(x_ref, g_ref, w_ref, gq_ref, gk_ref, gmq_ref, cos_ref, sin_ref, p16_ref,
                   f_ref, qkv1_ref, qkv4_ref, qkv16_ref, mq_ref, nat_ref, *, tm, widths, col_chunk):
    FW, AW, MW = widths
    h = _rms(x_ref[...], g_ref[...]).astype(BF16)
    cos = cos_ref[...]
    sin = sin_ref[...]
    slabs = nat_ref.shape[0]

    def heads(col0, width, emit, fn):
        cc = min(col_chunk, width)
        for c0 in range(0, width, cc):
            p = _dot(h, w_ref[:, col0 + c0:col0 + c0 + cc])
            for j in range(cc // HEAD_DIM):
                emit(c0 // HEAD_DIM + j, fn(p[:, j * HEAD_DIM:(j + 1) * HEAD_DIM]))

    def emit_f(head, t):
        f_ref[:, head * HEAD_DIM:(head + 1) * HEAD_DIM] = t

    def class_copies(base):
        def emit(head, t):
            sl = slice((base + head) * HEAD_DIM, (base + head + 1) * HEAD_DIM)
            slab = (base + head) % slabs
            qkv1_ref[:, sl] = t.astype(BF16)
            nat_ref[slab] = t
            for r4 in range(4):
                qkv4_ref[r4, :, sl] = nat_ref[slab, pl.ds(r4, tm // 4, stride=4), :].astype(BF16)
        return emit

    def regroup_16(part):
        cols = slice(part * AW, (part + 1) * AW)
        rows16 = tm // 16
        for r4 in range(4):
            y = _dot(p16_ref[...], qkv4_ref[r4, :, cols])
            for c in range(4):
                qkv16_ref[4 * c + r4, :, cols] = y[c * rows16:(c + 1) * rows16].astype(BF16)

    def norm_rope(g_ref_, post_scale):
        def fn(t):
            t = _rms(t, g_ref_[...])
            t = t * cos + pltpu.roll(t, HEAD_DIM // 2, 1) * sin
            return t if post_scale is None else t * post_scale
        return fn

    def emit_mq(head, t):
        mq_ref[:, head * HEAD_DIM:(head + 1) * HEAD_DIM] = t.astype(BF16)

    nh = AW // HEAD_DIM
    heads(0, FW, emit_f, lambda t: t)
    heads(FW, AW, class_copies(0), norm_rope(gq_ref, Q_SCALE))
    heads(FW + AW, AW, class_copies(nh), norm_rope(gk_ref, None))
    regroup_16(0)
    heads(FW + 2 * AW, AW, class_copies(2 * nh), lambda t: t)
    regroup_16(1)
    heads(FW + 3 * AW, MW, emit_mq, lambda t: _rms(t, gmq_ref[...]) * Q_SCALE)
    regroup_16(2)


def _inproj(x, g, w, l, widths, gq, gk, gmq, cos2, sin2, *, tm=512):
    B, S, D = x.shape
    FW, AW, MW = widths
    assert S % tm == 0 and tm % 256 == 0
    row = lambda w_: pl.BlockSpec((None, tm, w_), lambda b, i: (b, i, 0))
    cls = lambda d: pl.BlockSpec((None, d, tm // d, 3 * AW), lambda b, i: (b, 0, i, 0))
    vec = lambda n: pl.BlockSpec((1, n), lambda b, i: (0, 0))
    tab = pl.BlockSpec((tm, HEAD_DIM), lambda b, i: (i, 0))
    nat = jax.ShapeDtypeStruct((B, S, 3 * AW), BF16)
    c4 = jax.ShapeDtypeStruct((B, 4, S // 4, 3 * AW), BF16)
    c16 = jax.ShapeDtypeStruct((B, 16, S // 16, 3 * AW), BF16)
    n = jnp.arange(tm // 4, dtype=jnp.int32)
    p16 = jax.nn.one_hot(4 * (n % (tm // 16)) + n // (tm // 16), tm // 4, dtype=BF16)
    return pl.pallas_call(
        functools.partial(_inproj_kernel, tm=tm, widths=widths, col_chunk=512),
        grid=(B, S // tm),
        in_specs=[row(D), vec(D),
                  pl.BlockSpec((None,) + w.shape[1:], lambda b, i: (l, 0, 0), pipeline_mode=pl.Buffered(1)),
                  vec(HEAD_DIM), vec(HEAD_DIM), vec(HEAD_DIM), tab, tab,
                  pl.BlockSpec(p16.shape, lambda b, i: (0, 0))],
        out_specs=[row(FW), row(3 * AW), cls(4), cls(16), row(MW)],
        out_shape=[jax.ShapeDtypeStruct((B, S, FW), F32), nat, c4, c16, jax.ShapeDtypeStruct((B, S, MW), BF16)],
        scratch_shapes=[pltpu.VMEM((INPROJ_SLABS, tm, HEAD_DIM), F32)],
        compiler_params=_params("parallel", "parallel"),
        name="inproj",
    )(x, g.reshape(1, D), w, gq.reshape(1, -1), gk.reshape(1, -1), gmq.reshape(1, -1), cos2, sin2, p16)


def _fft1_kernel(k_ref, x_ref, o_ref):
    n1, rows, c = x_ref.shape
    r = FFT_ROWS
    for j in range(0, rows, r):
        x = x_ref[:, j:j + r, :].reshape(n1 * r, c).astype(BF16)
        o_ref[:, :, j:j + r, :] = _dot(k_ref[...], x).reshape(2, n1, r, c)


def _fft1(f_in, k1r):
    B, S, C = f_in.shape
    N2 = FFT_N2
    N1 = S // N2
    R = FFT1_ROWS_PER_STEP
    return pl.pallas_call(
        _fft1_kernel,
        grid=(B, N2 // R),
        in_specs=[pl.BlockSpec(k1r.shape, lambda b, j: (0, 0)),
                  pl.BlockSpec((None, N1, R, C), lambda b, j: (b, 0, j, 0))],
        out_specs=pl.BlockSpec((None, 2, N1, R, C), lambda b, j: (b, 0, 0, j, 0)),
        out_shape=jax.ShapeDtypeStruct((B, 2, N1, N2, C), F32),
        compiler_params=_params("parallel", "parallel"),
        name="fft1",
    )(k1r, f_in.reshape(B, N1, N2, C))


def _fft2_kernel(y_ref, tc_ref, ts_ref, m2_ref, p_ref, cbd_ref, sbd_ref, wf_ref, go_ref, o_ref, *, G, N2, scale):
    c = y_ref.shape[-1]
    reps = c // HEAD_DIM
    br, bi = y_ref[0], y_ref[1]
    tc = jnp.concatenate([tc_ref[...]] * reps, axis=1)
    ts = jnp.concatenate([ts_ref[...]] * reps, axis=1)
    pr = (br * tc + bi * ts).astype(BF16)
    pi = (bi * tc - br * ts).astype(BF16)
    blocks = lambda t: [t[g * N2:(g + 1) * N2] for g in range(G)]
    stacked = jnp.concatenate([jnp.concatenate(blocks(pr), axis=1), jnp.concatenate(blocks(pi), axis=1)], axis=0)
    z = _dot(m2_ref[...], stacked)
    rows = lambda t: jnp.concatenate([t[:, g * c:(g + 1) * c] for g in range(G)], axis=0)
    zr, zi = rows(z[:N2]).astype(BF16), rows(z[N2:]).astype(BF16)
    y = _dot(zr, cbd_ref[...]) + _dot(zi, sbd_ref[...])
    yf = _dot((y * scale).astype(BF16), wf_ref[...])
    o = _rms(yf, go_ref[...]).astype(BF16)
    o_ref[...] = _dot(p_ref[...], o).reshape(N2, G, c)


def _fft2(y1, tc, ts, m2, cbd, sbd, wf, l, go):
    B, _, N1, N2, C = y1.shape
    G = FFT_ROWS
    S = N1 * N2
    rows = G * N2
    full = lambda a: pl.BlockSpec(a.shape, lambda b, kb: (0,) * a.ndim)
    tw = pl.BlockSpec((rows, HEAD_DIM), lambda b, kb: (kb, 0))
    t = jnp.arange(rows, dtype=jnp.int32)
    perm = jax.nn.one_hot((t % G) * N2 + t // G, rows, dtype=BF16)
    out = pl.pallas_call(
        functools.partial(_fft2_kernel, G=G, N2=N2, scale=1.0 / math.sqrt(S * HEAD_DIM)),
        grid=(B, N1 // G),
        in_specs=[pl.BlockSpec((None, 2, rows, C), lambda b, kb: (b, 0, kb, 0)), tw, tw,
                  full(m2), full(perm), full(cbd), full(sbd),
                  pl.BlockSpec((None, C, C), lambda b, kb: (l, 0, 0)),
                  pl.BlockSpec((1, C), lambda b, kb: (0, 0))],
        out_specs=pl.BlockSpec((None, N2, G, C), lambda b, kb: (b, 0, kb, 0)),
        out_shape=jax.ShapeDtypeStruct((B, N2, N1, C), F32),
        compiler_params=_params("parallel", "parallel"),
        name="fft2",
    )(y1.reshape(B, 2, S, C), tc, ts, m2, perm, cbd, sbd, wf, go.reshape(1, C))
    return out.reshape(B, S, C)


def _dft_tables(S, C):
    N2 = FFT_N2
    N1 = S // N2
    R = FFT_ROWS

    def cs(n):
        j = jnp.arange(n, dtype=jnp.int32)
        ang = ((j[:, None] * j[None, :]) % n).astype(F32) * (2.0 * math.pi / n)
        return jnp.cos(ang), jnp.sin(ang)

    c1, s1 = cs(N1)
    f1 = jnp.concatenate([c1, -s1], axis=0)
    ri = jnp.arange(2 * N1 * R, dtype=jnp.int32)[:, None]
    ci = jnp.arange(N1 * R, dtype=jnp.int32)[None, :]
    pick_row = (ri // R == jnp.arange(2 * N1, dtype=jnp.int32)[None, :]).astype(F32)
    pick_col = (jnp.arange(N1, dtype=jnp.int32)[:, None] == ci // R).astype(F32)
    spread = jnp.dot(jnp.dot(pick_row, f1, precision=lax.Precision.HIGHEST), pick_col,
                     precision=lax.Precision.HIGHEST)
    k1r = jnp.where(ri % R == ci % R, spread, 0.0).astype(BF16)
    c2, s2 = cs(N2)
    m2 = jnp.concatenate([jnp.concatenate([c2, s2], axis=1),
                          jnp.concatenate([-s2, c2], axis=1)], axis=0).astype(BF16)
    cc, sc = cs(HEAD_DIM)
    eye_g = jnp.eye(C // HEAD_DIM, dtype=F32)
    cbd = jnp.kron(eye_g, cc).astype(BF16)
    sbd = jnp.kron(eye_g, sc).astype(BF16)
    k1 = jnp.arange(N1, dtype=jnp.int32)[:, None]
    n2 = jnp.arange(N2, dtype=jnp.int32)[None, :]
    ang = (k1 * n2).astype(F32).reshape(S, 1) * (2.0 * math.pi / S)
    tc = jnp.broadcast_to(jnp.cos(ang), (S, HEAD_DIM))
    ts = jnp.broadcast_to(jnp.sin(ang), (S, HEAD_DIM))
    return k1r, m2, cbd, sbd, tc, ts


def _band_kernel(*refs, L, heads, qb, seqs, halo):
    if halo:
        q_ref, k_ref, v_ref, kp_ref, vp_ref, kn_ref, vn_ref, o_ref, st_ref = refs
    else:
        q_ref, k_ref, v_ref, o_ref, st_ref = refs
    step = pl.program_id(1)
    ones = jnp.ones((K_WINDOW, HEAD_DIM), BF16)
    lane = lax.broadcasted_iota(jnp.int32, (Q_BLOCK, HEAD_DIM), 1)
    col = lax.broadcasted_iota(jnp.int32, (Q_BLOCK, K_WINDOW), 1)
    row_minus_col = lax.broadcasted_iota(jnp.int32, (Q_BLOCK, K_WINDOW), 0) - col

    def window(main_ref, prev_ref, next_ref, b, sl):
        lo, hi = b * Q_BLOCK - HALF_WINDOW, b * Q_BLOCK + K_WINDOW - HALF_WINDOW
        parts = []
        if lo < 0:
            parts.append(prev_ref[:, sl])
        parts.append(main_ref[max(lo, 0):min(hi, qb * Q_BLOCK), sl])
        if hi > qb * Q_BLOCK:
            parts.append(next_ref[:, sl])
        return parts[0] if len(parts) == 1 else jnp.concatenate(parts, axis=0)

    for s in range(seqs):
        for b in range(qb):
            rows = slice(b * Q_BLOCK, (b + 1) * Q_BLOCK)
            if halo:
                key = (step * qb + b) * Q_BLOCK - HALF_WINDOW + col
                ok = (jnp.abs(row_minus_col + HALF_WINDOW) <= HALF_WINDOW) & (key >= 0) & (key < L)
                q_s, o_s, st_s = q_ref, o_ref, st_ref
            else:
                ws = min(max(b * Q_BLOCK - HALF_WINDOW, 0), L - K_WINDOW)
                ok = jnp.abs(row_minus_col + (b * Q_BLOCK - ws)) <= HALF_WINDOW
                q_s, o_s, st_s = q_ref.at[s], o_ref.at[s], st_ref.at[s]
            bias = jnp.where(ok, 0.0, -jnp.inf)
            stats = jnp.where(lane < STAT_SPLIT, 0.0, 1.0)
            for h in range(heads):
                sl = slice(h * HEAD_DIM, (h + 1) * HEAD_DIM)
                if halo:
                    k = window(k_ref, kp_ref, kn_ref, b, sl)
                    v = window(v_ref, vp_ref, vn_ref, b, sl)
                else:
                    k = k_ref[s, ws:ws + K_WINDOW, sl]
                    v = v_ref[s, ws:ws + K_WINDOW, sl]
                sc = lax.dot_general(q_s[rows, sl], k, (((1,), (1,)), ((), ())),
                                     preferred_element_type=F32) + bias
                m = jnp.max(sc, axis=-1, keepdims=True)
                p = jnp.exp2(sc - m).astype(BF16)
                oa = _dot(p, jnp.concatenate([v, ones], axis=1))
                o_s[rows, sl] = oa[:, :HEAD_DIM].astype(BF16)
                stats = jnp.where(lane == h, m, jnp.where(lane == STAT_SPLIT + h, oa[:, HEAD_DIM:], stats))
            st_s[rows, :] = stats


def _band_attention(qkv):
    G, L, AW = qkv.shape[0], qkv.shape[1], qkv.shape[2] // 3
    heads = AW // HEAD_DIM
    nblk = L // Q_BLOCK
    assert L % Q_BLOCK == 0 and L >= K_WINDOW and heads <= STAT_SPLIT
    qb = max(1, min(HEAD_BLOCKS_PER_STEP // heads, nblk))
    assert nblk % qb == 0
    out_shape = [jax.ShapeDtypeStruct((G, L, AW), BF16), jax.ShapeDtypeStruct((G, L, HEAD_DIM), F32)]
    if qb == nblk:
        seqs = max(1, min(HEAD_BLOCKS_PER_STEP // (heads * qb), G))
        assert G % seqs == 0
        blk = lambda w, c: pl.BlockSpec((seqs, L, w), lambda g, n: (g, 0, c))
        return pl.pallas_call(
            functools.partial(_band_kernel, L=L, heads=heads, qb=qb, seqs=seqs, halo=False),
            grid=(G // seqs, 1),
            in_specs=[blk(AW, 0), blk(AW, 1), blk(AW, 2)],
            out_specs=[blk(AW, 0), blk(HEAD_DIM, 0)],
            out_shape=out_shape,
            compiler_params=_params("parallel", "parallel"),
            name="band_attention",
        )(qkv, qkv, qkv)
    main = lambda c: pl.BlockSpec((None, qb * Q_BLOCK, AW), lambda g, n: (g, n, c))
    halo_per_step = qb * Q_BLOCK // HALF_WINDOW
    prev = lambda c: pl.BlockSpec((None, HALF_WINDOW, AW),
                                  lambda g, n: (g, jnp.maximum(n * halo_per_step - 1, 0), c))
    nxt = lambda c: pl.BlockSpec((None, HALF_WINDOW, AW),
                                 lambda g, n: (g, jnp.minimum((n + 1) * halo_per_step, L // HALF_WINDOW - 1), c))
    return pl.pallas_call(
        functools.partial(_band_kernel, L=L, heads=heads, qb=qb, seqs=1, halo=True),
        grid=(G, nblk // qb),
        in_specs=[main(0), main(1), main(2), prev(1), prev(2), nxt(1), nxt(2)],
        out_specs=[main(0), pl.BlockSpec((None, qb * Q_BLOCK, HEAD_DIM), lambda g, n: (g, n, 0))],
        out_shape=out_shape,
        compiler_params=_params("parallel", "parallel"),
        name="band_attention",
    )(*[qkv] * 7)


def _memkv_kernel(m_ref, g_ref, w_ref, gk_ref, k_ref, v_ref, *, MW):
    h = _rms(m_ref[...], g_ref[...]).astype(BF16)
    kv = _dot(h, w_ref[...])
    for j in range(MW // HEAD_DIM):
        sl = slice(j * HEAD_DIM, (j + 1) * HEAD_DIM)
        k_ref[:, sl] = _rms(kv[:, sl], gk_ref[...]).astype(BF16)
    v_ref[...] = kv[:, MW:].astype(BF16)


def _memkv(mem, g, w, l, gk, *, tm=256):
    B, M, D = mem.shape
    MW = w.shape[2] // 2
    rows = B * M
    tm = min(tm, rows)
    assert rows % tm == 0
    out = pl.BlockSpec((tm, MW), lambda i: (i, 0))
    return pl.pallas_call(
        functools.partial(_memkv_kernel, MW=MW),
        grid=(rows // tm,),
        in_specs=[pl.BlockSpec((tm, D), lambda i: (i, 0)),
                  pl.BlockSpec((1, D), lambda i: (0, 0)),
                  pl.BlockSpec((None,) + w.shape[1:], lambda i: (l, 0, 0)),
                  pl.BlockSpec((1, HEAD_DIM), lambda i: (0, 0))],
        out_specs=[out, out],
        out_shape=[jax.ShapeDtypeStruct((rows, MW), BF16)] * 2,
        compiler_params=_params("parallel"),
        name="memkv",
    )(mem.reshape(rows, D), g.reshape(1, D), w, gk.reshape(1, -1))


def _memory_attention(q_ref, k_ref, v_ref, go_ref):
    ones = jnp.ones((k_ref.shape[0], HEAD_DIM), BF16)
    parts = []
    for j in range(q_ref.shape[1] // HEAD_DIM):
        sl = slice(j * HEAD_DIM, (j + 1) * HEAD_DIM)
        s = lax.dot_general(q_ref[:, sl], k_ref[:, sl], (((1,), (1,)), ((), ())), preferred_element_type=F32)
        p = jnp.exp2(s - jnp.max(s, axis=-1, keepdims=True)).astype(BF16)
        oa = _dot(p, jnp.concatenate([v_ref[:, sl], ones], axis=1))
        parts.append(oa[:, :HEAD_DIM] * (1.0 / oa[:, HEAD_DIM:]))
    return _rms(jnp.concatenate(parts, axis=1), go_ref[...]).astype(BF16)


def _outproj_kernel(x_ref, yf_ref, o1_ref, o4_ref, o16_ref, l1_ref, l4_ref, l16_ref, mq_ref, mk_ref, mv_ref,
                    go_ref, gom_ref, p4_ref, p16_ref, e_ref, w_ref, out_ref, s4_ref, s16_ref, *, tm, AW):
    FW = yf_ref.shape[-1]
    ym = _memory_attention(mq_ref, mk_ref, mv_ref, gom_ref)
    for dil, src, dst in ((4, l4_ref, s4_ref), (16, l16_ref, s16_ref)):
        for r in range(dil):
            dst[pl.ds(r, tm // dil, stride=dil), :] = src[r]
    o1 = o1_ref[...]
    o4 = _dot(p4_ref[...], o4_ref[...].reshape(tm, AW))
    o16 = _dot(p16_ref[...], o16_ref[...].reshape(tm, AW))

    low = lax.broadcasted_iota(jnp.int32, (tm, HEAD_DIM), 1) < STAT_SPLIT

    def split(st):
        return jnp.where(low, st, 0.0), jnp.where(low, pltpu.roll(st, STAT_SPLIT, 1), 1.0)

    (m1, d1), (m4, d4), (m16, d16) = split(l1_ref[...]), split(s4_ref[...]), split(s16_ref[...])
    top = jnp.maximum(jnp.maximum(m1, m4), m16)
    e1, e4, e16 = jnp.exp2(m1 - top), jnp.exp2(m4 - top), jnp.exp2(m16 - top)
    inv = 1.0 / (e1 * d1 + e4 * d4 + e16 * d16)

    def spread(w):
        hi = w.astype(BF16)
        lo = (w - hi.astype(F32)).astype(BF16)
        return _dot(jnp.concatenate([hi, lo], axis=1), e_ref[...])

    y = spread(e1 * inv) * o1 + spread(e4 * inv) * o4 + spread(e16 * inv) * o16
    ya = _rms(y, go_ref[...]).astype(BF16)
    out_ref[...] = (x_ref[...] + _dot(yf_ref[...].astype(BF16), w_ref[:FW]) + _dot(ya, w_ref[FW:FW + AW])
                    + _dot(ym, w_ref[FW + AW:]))


def _unpermute_matrix(tm, dil):
    t = jnp.arange(tm, dtype=jnp.int32)
    return jax.nn.one_hot((t % dil) * (tm // dil) + t // dil, tm, dtype=BF16)


def _outproj(x, yf, o, lse, mq, mk, mv, go_a, go_m, w, l, *, tm=256):
    B, S, D = x.shape
    FW, AW, MW = yf.shape[-1], o[0].shape[-1], mq.shape[-1]
    M = mk.shape[1]
    assert S % tm == 0 and tm % 256 == 0 and all(t.shape[-1] == HEAD_DIM for t in lse)
    row = lambda w_: pl.BlockSpec((None, tm, w_), lambda b, i: (b, i, 0))
    cls = lambda d, w_: pl.BlockSpec((None, d, tm // d, w_), lambda b, i: (b, 0, i, 0))
    full = lambda a: pl.BlockSpec(a.shape, lambda b, i: (0,) * a.ndim)
    mem = pl.BlockSpec((None, M, MW), lambda b, i: (b, 0, 0))
    p4, p16 = _unpermute_matrix(tm, 4), _unpermute_matrix(tm, 16)
    spread = (jnp.arange(2 * HEAD_DIM, dtype=jnp.int32)[:, None] % HEAD_DIM
              == jnp.arange(AW, dtype=jnp.int32)[None, :] // HEAD_DIM).astype(BF16)
    return pl.pallas_call(
        functools.partial(_outproj_kernel, tm=tm, AW=AW),
        grid=(B, S // tm),
        in_specs=[row(D), row(FW), row(AW), cls(4, AW), cls(16, AW), row(HEAD_DIM), cls(4, HEAD_DIM),
                  cls(16, HEAD_DIM), row(MW), mem, mem, pl.BlockSpec((1, AW), lambda b, i: (0, 0)),
                  pl.BlockSpec((1, MW), lambda b, i: (0, 0)), full(p4), full(p16), full(spread),
                  pl.BlockSpec((None,) + w.shape[1:], lambda b, i: (l, 0, 0))],
        out_specs=row(D),
        out_shape=jax.ShapeDtypeStruct((B, S, D), F32),
        scratch_shapes=[pltpu.VMEM((tm, HEAD_DIM), F32)] * 2,
        compiler_params=_params("parallel", "parallel"),
        name="outproj",
    )(x, yf, *o, *lse, mq, mk, mv, go_a.reshape(1, AW), go_m.reshape(1, MW), p4, p16, spread, w)


def _rope_tables(S):
    inv = ROPE_THETA ** (-jnp.arange(0, HEAD_DIM, 2, dtype=F32) / HEAD_DIM)
    ang = jnp.arange(S, dtype=F32)[:, None] * inv[None, :]
    cos, sin = jnp.cos(ang), jnp.sin(ang)
    return jnp.concatenate([cos, cos], axis=1), jnp.concatenate([-sin, sin], axis=1)


def _trunk(x, mem, w, g, widths):
    B, S, D = x.shape
    FW, AW, MW = widths
    M = mem.shape[1]
    cos2, sin2 = _rope_tables(S)
    k1r, m2, cbd, sbd, tc, ts = _dft_tables(S, FW)
    for l in range(w["in"].shape[0]):
        x = _ffn(x.reshape(B * S, D), g["ffn1"][l], *w["ffn1"], l).reshape(B, S, D)

        f_in, *qkv, mq = _inproj(x, g["mix"][l], w["in"], l, widths, g["q"][l], g["k"][l], g["mq"][l],
                                 cos2, sin2)

        g_out = g["out"][l]
        y_f = _fft2(_fft1(f_in, k1r), tc, ts, m2, cbd, sbd, w["fourier"], l, g_out[:FW])

        outs, lses = [], []
        for t, dil in zip(qkv, DILATIONS):
            o, lse = _band_attention(t.reshape(B * dil, S // dil, 3 * AW))
            lead = (B, S) if dil == 1 else (B, dil, S // dil)
            outs.append(o.reshape(*lead, AW))
            lses.append(lse.reshape(*lead, HEAD_DIM))

        mk, mv = _memkv(mem, g["mem"][l], w["mem_kv"], l, g["mk"][l])
        x = _outproj(x, y_f, outs, lses, mq, mk.reshape(B, M, MW), mv.reshape(B, M, MW),
                     g_out[FW:FW + AW], g_out[FW + AW:], w["out"], l)
        x = _ffn(x.reshape(B * S, D), g["ffn2"][l], *w["ffn2"], l).reshape(B, S, D)
    return x


def kernel(x_prompt, x_sample, mem_prompt, mem_sample, g_ffn1, w_ffn1_gate, w_ffn1_up, w_ffn1_down, g_mix, w_in, g_q, g_k, g_mem, w_mem_kv, g_mq, g_mk, w_fourier, g_out, w_out, g_ffn2, w_ffn2_gate, w_ffn2_up, w_ffn2_down):
    FW = w_fourier.shape[2]
    MW = w_mem_kv.shape[2] // 2
    AW = (w_in.shape[2] - FW - MW) // 3
    b16 = lambda a: a.astype(BF16)
    w = {"ffn1": (b16(w_ffn1_gate), b16(w_ffn1_up), b16(w_ffn1_down)),
         "ffn2": (b16(w_ffn2_gate), b16(w_ffn2_up), b16(w_ffn2_down)),
         "in": b16(w_in), "out": b16(w_out), "mem_kv": b16(w_mem_kv), "fourier": b16(w_fourier)}
    g = {"ffn1": g_ffn1, "ffn2": g_ffn2, "mix": g_mix, "q": g_q, "k": g_k, "mq": g_mq, "mk": g_mk,
         "mem": g_mem, "out": g_out}
    widths = (FW, AW, MW)
    return (_trunk(x_prompt, mem_prompt, w, g, widths), _trunk(x_sample, mem_sample, w, g, widths))
```

```python
import functools
import math

import jax
import jax.numpy as jnp
from jax import lax
from jax.experimental import pallas as pl
from jax.experimental.pallas import tpu as pltpu

EPS = 1e-6
HEAD_DIM = 128
ROPE_THETA = 10000.0
DILATIONS = (1, 4, 16)
HALF_WINDOW = 64
Q_BLOCK = 2 * HALF_WINDOW
K_WINDOW = 2 * Q_BLOCK
FFT_N2 = 64
FFT_ROWS = 8
Q_SCALE = HEAD_DIM ** -0.5 * math.log2(math.e)
FFT1_ROWS_PER_STEP = 16
FFT2_GROUPS_PER_STEP = 2
HEAD_BLOCKS_PER_STEP = 128
STAT_SPLIT = HEAD_DIM // 2
INPROJ_SLABS = 8

BF16 = jnp.bfloat16
F32 = jnp.float32

VMEM_LIMIT_BYTES = 56 * 1024 * 1024


def _params(*semantics):
    return pltpu.CompilerParams(dimension_semantics=semantics, vmem_limit_bytes=VMEM_LIMIT_BYTES)


def _rms(x, g):
    return x * lax.rsqrt(jnp.mean(x * x, axis=-1, keepdims=True) + EPS) * g


def _dot(a, b):
    return jnp.dot(a, b, preferred_element_type=F32)


def _ffn_kernel(x_ref, g_ref, wg_ref, wu_ref, wd_ref, o_ref, h_ref):
    f = pl.program_id(1)

    @pl.when(f == 0)
    def _():
        x = x_ref[...]
        h_ref[...] = _rms(x, g_ref[...]).astype(BF16)
        o_ref[...] = x

    h = h_ref[...]
    gate = _dot(h, wg_ref[...])
    up = _dot(h, wu_ref[...])
    a = (0.5 * gate * (1.0 / (1.0 + jnp.exp(-gate))) * up).astype(BF16)
    o_ref[...] += _dot(a, wd_ref[...])


def _ffn(x, g, wg, wu, wd, l, *, tm=1024, tf=512):
    T, D = x.shape
    F = wg.shape[2]
    tf = min(tf, F)
    assert T % tm == 0 and F % tf == 0
    return pl.pallas_call(
        _ffn_kernel,
        grid=(T // tm, F // tf),
        in_specs=[
            pl.BlockSpec((tm, D), lambda i, f: (i, 0)),
            pl.BlockSpec((1, D), lambda i, f: (0, 0)),
            pl.BlockSpec((None, D, tf), lambda i, f: (l, 0, f)),
            pl.BlockSpec((None, D, tf), lambda i, f: (l, 0, f)),
            pl.BlockSpec((None, tf, D), lambda i, f: (l, f, 0)),
        ],
        out_specs=pl.BlockSpec((tm, D), lambda i, f: (i, 0)),
        out_shape=jax.ShapeDtypeStruct((T, D), F32),
        scratch_shapes=[pltpu.VMEM((tm, D), BF16)],
        compiler_params=_params("parallel", "arbitrary"),
        name="ffn",
    )(x, g.reshape(1, D), wg, wu, wd)


def _inproj_kernel(x_ref, g_ref, w_ref, gq_ref, gk_ref, gmq_ref, cos_ref, sin_ref, p16_ref,
                   f_ref, qkv1_ref, qkv4_ref, qkv16_ref, mq_ref, nat_ref, *, tm, widths, col_chunk):
    FW, AW, MW = widths
    h = _rms(x_ref[...], g_ref[...]).astype(BF16)
    cos = cos_ref[...]
    sin = sin_ref[...]
    slabs = nat_ref.shape[0]

    def heads(col0, width, emit, fn):
        cc = min(col_chunk, width)
        for c0 in range(0, width, cc):
            p = _dot(h, w_ref[:, col0 + c0:col0 + c0 + cc])
            for j in range(cc // HEAD_DIM):
                emit(c0 // HEAD_DIM + j, fn(p[:, j * HEAD_DIM:(j + 1) * HEAD_DIM]))

    def emit_f(head, t):
        f_ref[:, head * HEAD_DIM:(head + 1) * HEAD_DIM] = t

    def class_copies(base):
        def emit(head, t):
            sl = slice((base + head) * HEAD_DIM, (base + head + 1) * HEAD_DIM)
            slab = (base + head) % slabs
            qkv1_ref[:, sl] = t.astype(BF16)
            nat_ref[slab] = t
            for r4 in range(4):
                qkv4_ref[r4, :, sl] = nat_ref[slab, pl.ds(r4, tm // 4, stride=4), :].astype(BF16)
        return emit

    def regroup_16(part):
        cols = slice(part * AW, (part + 1) * AW)
        rows16 = tm // 16
        for r4 in range(4):
            y = _dot(p16_ref[...], qkv4_ref[r4, :, cols])
            for c in range(4):
                qkv16_ref[4 * c + r4, :, cols] = y[c * rows16:(c + 1) * rows16].astype(BF16)

    def norm_rope(g_ref_, post_scale):
        def fn(t):
            t = _rms(t, g_ref_[...])
            t = t * cos + pltpu.roll(t, HEAD_DIM // 2, 1) * sin
            return t if post_scale is None else t * post_scale
        return fn

    def emit_mq(head, t):
        mq_ref[:, head * HEAD_DIM:(head + 1) * HEAD_DIM] = t.astype(BF16)

    nh = AW // HEAD_DIM
    heads(0, FW, emit_f, lambda t: t)
    heads(FW, AW, class_copies(0), norm_rope(gq_ref, Q_SCALE))
    heads(FW + AW, AW, class_copies(nh), norm_rope(gk_ref, None))
    regroup_16(0)
    heads(FW + 2 * AW, AW, class_copies(2 * nh), lambda t: t)
    regroup_16(1)
    heads(FW + 3 * AW, MW, emit_mq, lambda t: _rms(t, gmq_ref[...]) * Q_SCALE)
    regroup_16(2)


def _inproj(x, g, w, l, widths, gq, gk, gmq, cos2, sin2, *, tm=512):
    B, S, D = x.shape
    FW, AW, MW = widths
    assert S % tm == 0 and tm % 256 == 0
    row = lambda w_: pl.BlockSpec((None, tm, w_), lambda b, i: (b, i, 0))
    cls = lambda d: pl.BlockSpec((None, d, tm // d, 3 * AW), lambda b, i: (b, 0, i, 0))
    vec = lambda n: pl.BlockSpec((1, n), lambda b, i: (0, 0))
    tab = pl.BlockSpec((tm, HEAD_DIM), lambda b, i: (i, 0))
    nat = jax.ShapeDtypeStruct((B, S, 3 * AW), BF16)
    c4 = jax.ShapeDtypeStruct((B, 4, S // 4, 3 * AW), BF16)
    c16 = jax.ShapeDtypeStruct((B, 16, S // 16, 3 * AW), BF16)
    n = jnp.arange(tm // 4, dtype=jnp.int32)
    p16 = jax.nn.one_hot(4 * (n % (tm // 16)) + n // (tm // 16), tm // 4, dtype=BF16)
    return pl.pallas_call(
        functools.partial(_inproj_kernel, tm=tm, widths=widths, col_chunk=512),
        grid=(B, S // tm),
        in_specs=[row(D), vec(D),
                  pl.BlockSpec((None,) + w.shape[1:], lambda b, i: (l, 0, 0), pipeline_mode=pl.Buffered(1)),
                  vec(HEAD_DIM), vec(HEAD_DIM), vec(HEAD_DIM), tab, tab,
                  pl.BlockSpec(p16.shape, lambda b, i: (0, 0))],
        out_specs=[row(FW), row(3 * AW), cls(4), cls(16), row(MW)],
        out_shape=[jax.ShapeDtypeStruct((B, S, FW), F32), nat, c4, c16, jax.ShapeDtypeStruct((B, S, MW), BF16)],
        scratch_shapes=[pltpu.VMEM((INPROJ_SLABS, tm, HEAD_DIM), F32)],
        compiler_params=_params("parallel", "parallel"),
        name="inproj",
    )(x, g.reshape(1, D), w, gq.reshape(1, -1), gk.reshape(1, -1), gmq.reshape(1, -1), cos2, sin2, p16)


def _fft1_kernel(k_ref, x_ref, o_ref):
    n1, rows, c = x_ref.shape
    r = FFT_ROWS
    for j in range(0, rows, r):
        x = x_ref[:, j:j + r, :].reshape(n1 * r, c).astype(BF16)
        o_ref[:, :, j:j + r, :] = _dot(k_ref[...], x).reshape(2, n1, r, c)


def _fft1(f_in, k1r):
    B, S, C = f_in.shape
    N2 = FFT_N2
    N1 = S // N2
    R = FFT1_ROWS_PER_STEP
    return pl.pallas_call(
        _fft1_kernel,
        grid=(B, N2 // R),
        in_specs=[pl.BlockSpec(k1r.shape, lambda b, j: (0, 0)),
                  pl.BlockSpec((None, N1, R, C), lambda b, j: (b, 0, j, 0))],
        out_specs=pl.BlockSpec((None, 2, N1, R, C), lambda b, j: (b, 0, 0, j, 0)),
        out_shape=jax.ShapeDtypeStruct((B, 2, N1, N2, C), F32),
        compiler_params=_params("parallel", "parallel"),
        name="fft1",
    )(k1r, f_in.reshape(B, N1, N2, C))


def _fft2_kernel(y_ref, tc_ref, ts_ref, m2_ref, p_ref, cbd_ref, sbd_ref, wf_ref, go_ref, o_ref, *, G, N2, scale):
    c = y_ref.shape[-1]
    reps = c // HEAD_DIM
    blocks = lambda t: [t[g * N2:(g + 1) * N2] for g in range(G)]
    rows = lambda t: jnp.concatenate([t[:, g * c:(g + 1) * c] for g in range(G)], axis=0)
    for u in range(y_ref.shape[1] // (G * N2)):
        sl = slice(u * G * N2, (u + 1) * G * N2)
        br, bi = y_ref[0, sl, :], y_ref[1, sl, :]
        tc = jnp.concatenate([tc_ref[sl, :]] * reps, axis=1)
        ts = jnp.concatenate([ts_ref[sl, :]] * reps, axis=1)
        pr = (br * tc + bi * ts).astype(BF16)
        pi = (bi * tc - br * ts).astype(BF16)
        stacked = jnp.concatenate([jnp.concatenate(blocks(pr), axis=1), jnp.concatenate(blocks(pi), axis=1)],
                                  axis=0)
        z = _dot(m2_ref[...], stacked)
        zr, zi = rows(z[:N2]).astype(BF16), rows(z[N2:]).astype(BF16)
        y = _dot(zr, cbd_ref[...]) + _dot(zi, sbd_ref[...])
        yf = _dot((y * scale).astype(BF16), wf_ref[...])
        o = _rms(yf, go_ref[...]).astype(BF16)
        o_ref[:, u * G:(u + 1) * G, :] = _dot(p_ref[...], o).reshape(N2, G, c)


def _fft2(y1, tc, ts, m2, cbd, sbd, wf, l, go):
    B, _, N1, N2, C = y1.shape
    G = FFT_ROWS
    S = N1 * N2
    rows = G * N2
    U = FFT2_GROUPS_PER_STEP
    assert N1 % (U * G) == 0
    full = lambda a: pl.BlockSpec(a.shape, lambda b, kb: (0,) * a.ndim)
    tw = pl.BlockSpec((U * rows, HEAD_DIM), lambda b, kb: (kb, 0))
    t = jnp.arange(rows, dtype=jnp.int32)
    perm = jax.nn.one_hot((t % G) * N2 + t // G, rows, dtype=BF16)
    out = pl.pallas_call(
        functools.partial(_fft2_kernel, G=G, N2=N2, scale=1.0 / math.sqrt(S * HEAD_DIM)),
        grid=(B, N1 // (U * G)),
        in_specs=[pl.BlockSpec((None, 2, U * rows, C), lambda b, kb: (b, 0, kb, 0)), tw, tw,
                  full(m2), full(perm), full(cbd), full(sbd),
                  pl.BlockSpec((None, C, C), lambda b, kb: (l, 0, 0)),
                  pl.BlockSpec((1, C), lambda b, kb: (0, 0))],
        out_specs=pl.BlockSpec((None, N2, U * G, C), lambda b, kb: (b, 0, kb, 0)),
        out_shape=jax.ShapeDtypeStruct((B, N2, N1, C), F32),
        compiler_params=_params("parallel", "parallel"),
        name="fft2",
    )(y1.reshape(B, 2, S, C), tc, ts, m2, perm, cbd, sbd, wf, go.reshape(1, C))
    return out.reshape(B, S, C)


def _dft_tables(S, C):
    N2 = FFT_N2
    N1 = S // N2
    R = FFT_ROWS

    def cs(n):
        j = jnp.arange(n, dtype=jnp.int32)
        ang = ((j[:, None] * j[None, :]) % n).astype(F32) * (2.0 * math.pi / n)
        return jnp.cos(ang), jnp.sin(ang)

    c1, s1 = cs(N1)
    f1 = jnp.concatenate([c1, -s1], axis=0)
    ri = jnp.arange(2 * N1 * R, dtype=jnp.int32)[:, None]
    ci = jnp.arange(N1 * R, dtype=jnp.int32)[None, :]
    pick_row = (ri // R == jnp.arange(2 * N1, dtype=jnp.int32)[None, :]).astype(F32)
    pick_col = (jnp.arange(N1, dtype=jnp.int32)[:, None] == ci // R).astype(F32)
    spread = jnp.dot(jnp.dot(pick_row, f1, precision=lax.Precision.HIGHEST), pick_col,
                     precision=lax.Precision.HIGHEST)
    k1r = jnp.where(ri % R == ci % R, spread, 0.0).astype(BF16)
    c2, s2 = cs(N2)
    m2 = jnp.concatenate([jnp.concatenate([c2, s2], axis=1),
                          jnp.concatenate([-s2, c2], axis=1)], axis=0).astype(BF16)
    cc, sc = cs(HEAD_DIM)
    eye_g = jnp.eye(C // HEAD_DIM, dtype=F32)
    cbd = jnp.kron(eye_g, cc).astype(BF16)
    sbd = jnp.kron(eye_g, sc).astype(BF16)
    k1 = jnp.arange(N1, dtype=jnp.int32)[:, None]
    n2 = jnp.arange(N2, dtype=jnp.int32)[None, :]
    ang = (k1 * n2).astype(F32).reshape(S, 1) * (2.0 * math.pi / S)
    tc = jnp.broadcast_to(jnp.cos(ang), (S, HEAD_DIM))
    ts = jnp.broadcast_to(jnp.sin(ang), (S, HEAD_DIM))
    return k1r, m2, cbd, sbd, tc, ts


def _band_kernel(*refs, L, heads, qb, seqs, halo):
    if halo:
        q_ref, k_ref, v_ref, kp_ref, vp_ref, kn_ref, vn_ref, o_ref, st_ref = refs
    else:
        q_ref, k_ref, v_ref, o_ref, st_ref = refs
    step = pl.program_id(1)
    ones = jnp.ones((K_WINDOW, HEAD_DIM), BF16)
    lane = lax.broadcasted_iota(jnp.int32, (Q_BLOCK, HEAD_DIM), 1)
    col = lax.broadcasted_iota(jnp.int32, (Q_BLOCK, K_WINDOW), 1)
    row_minus_col = lax.broadcasted_iota(jnp.int32, (Q_BLOCK, K_WINDOW), 0) - col

    def window(main_ref, prev_ref, next_ref, b, sl):
        lo, hi = b * Q_BLOCK - HALF_WINDOW, b * Q_BLOCK + K_WINDOW - HALF_WINDOW
        parts = []
        if lo < 0:
            parts.append(prev_ref[:, sl])
        parts.append(main_ref[max(lo, 0):min(hi, qb * Q_BLOCK), sl])
        if hi > qb * Q_BLOCK:
            parts.append(next_ref[:, sl])
        return parts[0] if len(parts) == 1 else jnp.concatenate(parts, axis=0)

    for s in range(seqs):
        for b in range(qb):
            rows = slice(b * Q_BLOCK, (b + 1) * Q_BLOCK)
            if halo:
                key = (step * qb + b) * Q_BLOCK - HALF_WINDOW + col
                ok = (jnp.abs(row_minus_col + HALF_WINDOW) <= HALF_WINDOW) & (key >= 0) & (key < L)
                q_s, o_s, st_s = q_ref, o_ref, st_ref
            else:
                ws = min(max(b * Q_BLOCK - HALF_WINDOW, 0), L - K_WINDOW)
                ok = jnp.abs(row_minus_col + (b * Q_BLOCK - ws)) <= HALF_WINDOW
                q_s, o_s, st_s = q_ref.at[s], o_ref.at[s], st_ref.at[s]
            bias = jnp.where(ok, 0.0, -jnp.inf)
            stats = jnp.where(lane < STAT_SPLIT, 0.0, 1.0)
            for h in range(heads):
                sl = slice(h * HEAD_DIM, (h + 1) * HEAD_DIM)
                if halo:
                    k = window(k_ref, kp_ref, kn_ref, b, sl)
                    v = window(v_ref, vp_ref, vn_ref, b, sl)
                else:
                    k = k_ref[s, ws:ws + K_WINDOW, sl]
                    v = v_ref[s, ws:ws + K_WINDOW, sl]
                sc = lax.dot_general(q_s[rows, sl], k, (((1,), (1,)), ((), ())),
                                     preferred_element_type=F32) + bias
                m = jnp.max(sc, axis=-1, keepdims=True)
                p = jnp.exp2(sc - m).astype(BF16)
                oa = _dot(p, jnp.concatenate([v, ones], axis=1))
                o_s[rows, sl] = oa[:, :HEAD_DIM].astype(BF16)
                stats = jnp.where(lane == h, m, jnp.where(lane == STAT_SPLIT + h, oa[:, HEAD_DIM:], stats))
            st_s[rows, :] = stats


def _band_attention(qkv):
    G, L, AW = qkv.shape[0], qkv.shape[1], qkv.shape[2] // 3
    heads = AW // HEAD_DIM
    nblk = L // Q_BLOCK
    assert L % Q_BLOCK == 0 and L >= K_WINDOW and heads <= STAT_SPLIT
    qb = max(1, min(HEAD_BLOCKS_PER_STEP // heads, nblk))
    assert nblk % qb == 0
    out_shape = [jax.ShapeDtypeStruct((G, L, AW), BF16), jax.ShapeDtypeStruct((G, L, HEAD_DIM), F32)]
    if qb == nblk:
        seqs = max(1, min(HEAD_BLOCKS_PER_STEP // (heads * qb), G))
        assert G % seqs == 0
        blk = lambda w, c: pl.BlockSpec((seqs, L, w), lambda g, n: (g, 0, c))
        return pl.pallas_call(
            functools.partial(_band_kernel, L=L, heads=heads, qb=qb, seqs=seqs, halo=False),
            grid=(G // seqs, 1),
            in_specs=[blk(AW, 0), blk(AW, 1), blk(AW, 2)],
            out_specs=[blk(AW, 0), blk(HEAD_DIM, 0)],
            out_shape=out_shape,
            compiler_params=_params("parallel", "parallel"),
            name="band_attention",
        )(qkv, qkv, qkv)
    main = lambda c: pl.BlockSpec((None, qb * Q_BLOCK, AW), lambda g, n: (g, n, c))
    halo_per_step = qb * Q_BLOCK // HALF_WINDOW
    prev = lambda c: pl.BlockSpec((None, HALF_WINDOW, AW),
                                  lambda g, n: (g, jnp.maximum(n * halo_per_step - 1, 0), c))
    nxt = lambda c: pl.BlockSpec((None, HALF_WINDOW, AW),
                                 lambda g, n: (g, jnp.minimum((n + 1) * halo_per_step, L // HALF_WINDOW - 1), c))
    return pl.pallas_call(
        functools.partial(_band_kernel, L=L, heads=heads, qb=qb, seqs=1, halo=True),
        grid=(G, nblk // qb),
        in_specs=[main(0), main(1), main(2), prev(1), prev(2), nxt(1), nxt(2)],
        out_specs=[main(0), pl.BlockSpec((None, qb * Q_BLOCK, HEAD_DIM), lambda g, n: (g, n, 0))],
        out_shape=out_shape,
        compiler_params=_params("parallel", "parallel"),
        name="band_attention",
    )(*[qkv] * 7)


def _memkv_kernel(m_ref, g_ref, w_ref, gk_ref, k_ref, v_ref, *, MW):
    h = _rms(m_ref[...], g_ref[...]).astype(BF16)
    kv = _dot(h, w_ref[...])
    for j in range(MW // HEAD_DIM):
        sl = slice(j * HEAD_DIM, (j + 1) * HEAD_DIM)
        k_ref[:, sl] = _rms(kv[:, sl], gk_ref[...]).astype(BF16)
    v_ref[...] = kv[:, MW:].astype(BF16)


def _memkv(mem, g, w, l, gk, *, tm=256):
    B, M, D = mem.shape
    MW = w.shape[2] // 2
    rows = B * M
    tm = min(tm, rows)
    assert rows % tm == 0
    out = pl.BlockSpec((tm, MW), lambda i: (i, 0))
    return pl.pallas_call(
        functools.partial(_memkv_kernel, MW=MW),
        grid=(rows // tm,),
        in_specs=[pl.BlockSpec((tm, D), lambda i: (i, 0)),
                  pl.BlockSpec((1, D), lambda i: (0, 0)),
                  pl.BlockSpec((None,) + w.shape[1:], lambda i: (l, 0, 0)),
                  pl.BlockSpec((1, HEAD_DIM), lambda i: (0, 0))],
        out_specs=[out, out],
        out_shape=[jax.ShapeDtypeStruct((rows, MW), BF16)] * 2,
        compiler_params=_params("parallel"),
        name="memkv",
    )(mem.reshape(rows, D), g.reshape(1, D), w, gk.reshape(1, -1))


def _memory_attention(q_ref, k_ref, v_ref, go_ref):
    ones = jnp.ones((k_ref.shape[0], HEAD_DIM), BF16)
    parts = []
    for j in range(q_ref.shape[1] // HEAD_DIM):
        sl = slice(j * HEAD_DIM, (j + 1) * HEAD_DIM)
        s = lax.dot_general(q_ref[:, sl], k_ref[:, sl], (((1,), (1,)), ((), ())), preferred_element_type=F32)
        p = jnp.exp2(s - jnp.max(s, axis=-1, keepdims=True)).astype(BF16)
        oa = _dot(p, jnp.concatenate([v_ref[:, sl], ones], axis=1))
        parts.append(oa[:, :HEAD_DIM] * (1.0 / oa[:, HEAD_DIM:]))
    return _rms(jnp.concatenate(parts, axis=1), go_ref[...]).astype(BF16)


def _outproj_kernel(x_ref, yf_ref, o1_ref, o4_ref, o16_ref, l1_ref, l4_ref, l16_ref, mq_ref, mk_ref, mv_ref,
                    go_ref, gom_ref, p4_ref, p16_ref, e_ref, w_ref, out_ref, s4_ref, s16_ref, *, tm, AW):
    FW = yf_ref.shape[-1]
    ym = _memory_attention(mq_ref, mk_ref, mv_ref, gom_ref)
    for dil, src, dst in ((4, l4_ref, s4_ref), (16, l16_ref, s16_ref)):
        for r in range(dil):
            dst[pl.ds(r, tm // dil, stride=dil), :] = src[r]
    o1 = o1_ref[...]
    o4 = _dot(p4_ref[...], o4_ref[...].reshape(tm, AW))
    o16 = _dot(p16_ref[...], o16_ref[...].reshape(tm, AW))

    low = lax.broadcasted_iota(jnp.int32, (tm, HEAD_DIM), 1) < STAT_SPLIT

    def split(st):
        return jnp.where(low, st, 0.0), jnp.where(low, pltpu.roll(st, STAT_SPLIT, 1), 1.0)

    (m1, d1), (m4, d4), (m16, d16) = split(l1_ref[...]), split(s4_ref[...]), split(s16_ref[...])
    top = jnp.maximum(jnp.maximum(m1, m4), m16)
    e1, e4, e16 = jnp.exp2(m1 - top), jnp.exp2(m4 - top), jnp.exp2(m16 - top)
    inv = 1.0 / (e1 * d1 + e4 * d4 + e16 * d16)

    def spread(w):
        hi = w.astype(BF16)
        lo = (w - hi.astype(F32)).astype(BF16)
        return _dot(jnp.concatenate([hi, lo], axis=1), e_ref[...])

    y = spread(e1 * inv) * o1 + spread(e4 * inv) * o4 + spread(e16 * inv) * o16
    ya = _rms(y, go_ref[...]).astype(BF16)
    out_ref[...] = (x_ref[...] + _dot(yf_ref[...].astype(BF16), w_ref[:FW]) + _dot(ya, w_ref[FW:FW + AW])
                    + _dot(ym, w_ref[FW + AW:]))


def _unpermute_matrix(tm, dil):
    t = jnp.arange(tm, dtype=jnp.int32)
    return jax.nn.one_hot((t % dil) * (tm // dil) + t // dil, tm, dtype=BF16)


def _outproj(x, yf, o, lse, mq, mk, mv, go_a, go_m, w, l, *, tm=256):
    B, S, D = x.shape
    FW, AW, MW = yf.shape[-1], o[0].shape[-1], mq.shape[-1]
    M = mk.shape[1]
    assert S % tm == 0 and tm % 256 == 0 and all(t.shape[-1] == HEAD_DIM for t in lse)
    row = lambda w_: pl.BlockSpec((None, tm, w_), lambda b, i: (b, i, 0))
    cls = lambda d, w_: pl.BlockSpec((None, d, tm // d, w_), lambda b, i: (b, 0, i, 0))
    full = lambda a: pl.BlockSpec(a.shape, lambda b, i: (0,) * a.ndim)
    mem = pl.BlockSpec((None, M, MW), lambda b, i: (b, 0, 0))
    p4, p16 = _unpermute_matrix(tm, 4), _unpermute_matrix(tm, 16)
    spread = (jnp.arange(2 * HEAD_DIM, dtype=jnp.int32)[:, None] % HEAD_DIM
              == jnp.arange(AW, dtype=jnp.int32)[None, :] // HEAD_DIM).astype(BF16)
    return pl.pallas_call(
        functools.partial(_outproj_kernel, tm=tm, AW=AW),
        grid=(B, S // tm),
        in_specs=[row(D), row(FW), row(AW), cls(4, AW), cls(16, AW), row(HEAD_DIM), cls(4, HEAD_DIM),
                  cls(16, HEAD_DIM), row(MW), mem, mem, pl.BlockSpec((1, AW), lambda b, i: (0, 0)),
                  pl.BlockSpec((1, MW), lambda b, i: (0, 0)), full(p4), full(p16), full(spread),
                  pl.BlockSpec((None,) + w.shape[1:], lambda b, i: (l, 0, 0))],
        out_specs=row(D),
        out_shape=jax.ShapeDtypeStruct((B, S, D), F32),
        scratch_shapes=[pltpu.VMEM((tm, HEAD_DIM), F32)] * 2,
        compiler_params=_params("parallel", "parallel"),
        name="outproj",
    )(x, yf, *o, *lse, mq, mk, mv, go_a.reshape(1, AW), go_m.reshape(1, MW), p4, p16, spread, w)


def _rope_tables(S):
    inv = ROPE_THETA ** (-jnp.arange(0, HEAD_DIM, 2, dtype=F32) / HEAD_DIM)
    ang = jnp.arange(S, dtype=F32)[:, None] * inv[None, :]
    cos, sin = jnp.cos(ang), jnp.sin(ang)
    return jnp.concatenate([cos, cos], axis=1), jnp.concatenate([-sin, sin], axis=1)


def _trunk(x, mem, w, g, widths):
    B, S, D = x.shape
    FW, AW, MW = widths
    M = mem.shape[1]
    cos2, sin2 = _rope_tables(S)
    k1r, m2, cbd, sbd, tc, ts = _dft_tables(S, FW)
    for l in range(w["in"].shape[0]):
        x = _ffn(x.reshape(B * S, D), g["ffn1"][l], *w["ffn1"], l).reshape(B, S, D)

        f_in, *qkv, mq = _inproj(x, g["mix"][l], w["in"], l, widths, g["q"][l], g["k"][l], g["mq"][l],
                                 cos2, sin2)

        g_out = g["out"][l]
        y_f = _fft2(_fft1(f_in, k1r), tc, ts, m2, cbd, sbd, w["fourier"], l, g_out[:FW])

        outs, lses = [], []
        for t, dil in zip(qkv, DILATIONS):
            o, lse = _band_attention(t.reshape(B * dil, S // dil, 3 * AW))
            lead = (B, S) if dil == 1 else (B, dil, S // dil)
            outs.append(o.reshape(*lead, AW))
            lses.append(lse.reshape(*lead, HEAD_DIM))

        mk, mv = _memkv(mem, g["mem"][l], w["mem_kv"], l, g["mk"][l])
        x = _outproj(x, y_f, outs, lses, mq, mk.reshape(B, M, MW), mv.reshape(B, M, MW),
                     g_out[FW:FW + AW], g_out[FW + AW:], w["out"], l)
        x = _ffn(x.reshape(B * S, D), g["ffn2"][l], *w["ffn2"], l).reshape(B, S, D)
    return x


def kernel(x_prompt, x_sample, mem_prompt, mem_sample, g_ffn1, w_ffn1_gate, w_ffn1_up, w_ffn1_down, g_mix, w_in, g_q, g_k, g_mem, w_mem_kv, g_mq, g_mk, w_fourier, g_out, w_out, g_ffn2, w_ffn2_gate, w_ffn2_up, w_ffn2_down):
    FW = w_fourier.shape[2]
    MW = w_mem_kv.shape[2] // 2
    AW = (w_in.shape[2] - FW - MW) // 3
    b16 = lambda a: a.astype(BF16)
    w = {"ffn1": (b16(w_ffn1_gate), b16(w_ffn1_up), b16(w_ffn1_down)),
         "ffn2": (b16(w_ffn2_gate), b16(w_ffn2_up), b16(w_ffn2_down)),
         "in": b16(w_in), "out": b16(w_out), "mem_kv": b16(w_mem_kv), "fourier": b16(w_fourier)}
    g = {"ffn1": g_ffn1, "ffn2": g_ffn2, "mix": g_mix, "q": g_q, "k": g_k, "mq": g_mq, "mk": g_mk,
         "mem": g_mem, "out": g_out}
    widths = (FW, AW, MW)
    return (_trunk(x_prompt, mem_prompt, w, g, widths), _trunk(x_sample, mem_sample, w, g, widths))
```
